```python
import math
import jax, jax.numpy as jnp
from jax import lax
import numpy as np

D_MODEL = 4096
BATCH = 2
SEQ = 4096
DEPTH = 2

CHUNK = 64
HEAD_DIM = 128
EPS = 1e-6
SSM_WIDTH = 1024
SSM_GROUP = 16
SSM_GROUPS = SSM_WIDTH // SSM_GROUP
SSM_STATE = 64
CA_HEADS = 12
CA_WIDTH = CA_HEADS * HEAD_DIM
CA_LEFT_CHUNKS = 8
CA_BAND = (CA_LEFT_CHUNKS + 1) * CHUNK
REL_CLIP = 128
DA_HEADS = 6
DA_WIDTH = DA_HEADS * 2 * HEAD_DIM
Q_BLOCK = 128
N_BRANCH = 3
D_FF = 4 * D_MODEL
IN_WIDTH = SSM_WIDTH + 3 * CA_WIDTH + 3 * DA_WIDTH + N_BRANCH * D_MODEL

kernel_name = "hybrid_s5_chunkattn_diffattn_gated_block"


def rms_norm(x, gain):
    x32 = x.astype(jnp.float32)
    y = x32 * lax.rsqrt(jnp.mean(x32 * x32, axis=-1, keepdims=True) + EPS)
    return (y * gain.astype(jnp.float32)).astype(x.dtype)


def alibi_slopes(n_heads):
    return 2.0 ** (-8.0 * jnp.arange(1, n_heads + 1, dtype=jnp.float32) / n_heads)


def s5_mixer(u, a_re, a_im, log_dt, b_re, b_im, c_re, c_im, d_skip, w_glu, b_glu):
    bsz, seq, _ = u.shape
    f32 = jnp.float32
    lam = lax.complex(jnp.minimum(a_re.astype(f32), -1e-4), a_im.astype(f32))
    dt = jnp.exp(log_dt.astype(f32))[:, None]
    a_bar = jnp.exp(lam * dt)
    b = lax.complex(b_re.astype(f32), b_im.astype(f32))
    b_bar = ((a_bar - 1.0) / lam)[..., None] * b
    c = lax.complex(c_re.astype(f32), c_im.astype(f32))
    ug = u.astype(f32).reshape(bsz, seq, SSM_GROUPS, SSM_GROUP)
    bu = jnp.einsum('gph,bsgh->sbgp', b_bar, ug)
    a_seq = jnp.broadcast_to(a_bar, bu.shape)

    def combine(left, right):
        a_l, b_l = left
        a_r, b_r = right
        return a_r * a_l, a_r * b_l + b_r

    _, states = lax.associative_scan(combine, (a_seq, bu), axis=0)
    y = jnp.einsum('ghp,sbgp->bsgh', c, states).real.reshape(bsz, seq, SSM_WIDTH)
    y = y + d_skip.astype(f32) * u.astype(f32)
    z = jax.nn.gelu(y).astype(u.dtype)
    return z * jax.nn.sigmoid(z @ w_glu + b_glu)


def chunk_band_attention(q, k, v, q_gain, k_gain, rel_bias):
    bsz, seq, h, dh = q.shape
    f32 = jnp.float32
    nc = seq // CHUNK
    q = rms_norm(q, q_gain) * (dh ** -0.5)
    k = rms_norm(k, k_gain)
    pad = ((0, 0), (CA_LEFT_CHUNKS * CHUNK, 0), (0, 0), (0, 0))
    k_pad = jnp.pad(k, pad).reshape(bsz, nc + CA_LEFT_CHUNKS, CHUNK, h, dh)
    v_pad = jnp.pad(v, pad).reshape(bsz, nc + CA_LEFT_CHUNKS, CHUNK, h, dh)
    band_idx = jnp.arange(nc)[:, None] + jnp.arange(CA_LEFT_CHUNKS + 1)[None, :]
    k_band = k_pad[:, band_idx].reshape(bsz, nc, CA_BAND, h, dh)
    v_band = v_pad[:, band_idx].reshape(bsz, nc, CA_BAND, h, dh)
    qc = q.reshape(bsz, nc, CHUNK, h, dh)
    scores = jnp.einsum('bnqhd,bnkhd->bhnqk', qc, k_band).astype(f32)
    rel = CA_LEFT_CHUNKS * CHUNK + jnp.arange(CHUNK)[:, None] - jnp.arange(CA_BAND)[None, :]
    rel_idx = jnp.clip(rel, -REL_CLIP, REL_CLIP) + REL_CLIP
    bias = rel_bias.astype(f32)[:, rel_idx]
    key_pos = (jnp.arange(nc)[:, None] - CA_LEFT_CHUNKS) * CHUNK + jnp.arange(CA_BAND)[None, :]
    valid = key_pos >= 0
    scores = scores + bias[None, :, None]
    scores = jnp.where(valid[None, None, :, None, :], scores, -1e30)
    p = jax.nn.softmax(scores, axis=-1).astype(v.dtype)
    out = jnp.einsum('bhnqk,bnkhd->bnqhd', p, v_band)
    return out.reshape(bsz, seq, h * dh)


def diff_attention(q, k, v, q_gain, k_gain, lam_q1, lam_k1, lam_q2, lam_k2, subln_gain, lambda_init):
    bsz, seq, h, _, dh = q.shape
    f32 = jnp.float32
    q = rms_norm(q, q_gain) * (dh ** -0.5)
    k = rms_norm(k, k_gain)
    lam = (jnp.exp(jnp.sum(lam_q1.astype(f32) * lam_k1.astype(f32)))
           - jnp.exp(jnp.sum(lam_q2.astype(f32) * lam_k2.astype(f32))) + lambda_init)
    slopes = alibi_slopes(h)
    key_pos = jnp.arange(seq)
    nb = seq // Q_BLOCK
    q_blocks = q.reshape(bsz, nb, Q_BLOCK, h, 2, dh).transpose(1, 0, 2, 3, 4, 5)

    def block(args):
        qb, start = args
        q_pos = start + jnp.arange(Q_BLOCK)
        s = jnp.einsum('bqhcd,bkhcd->bhcqk', qb, k).astype(f32)
        dist = jnp.abs(q_pos[:, None] - key_pos[None, :]).astype(f32)
        s = s - slopes[None, :, None, None, None] * dist[None, None, None]
        allowed = (key_pos[None, :] // CHUNK) <= (q_pos[:, None] // CHUNK)
        s = jnp.where(allowed, s, -1e30)
        p = jax.nn.softmax(s, axis=-1)
        w = p[:, :, 0] - lam * p[:, :, 1]
        return jnp.einsum('bhqk,bkhe->bqhe', w.astype(v.dtype), v)

    out = lax.map(block, (q_blocks, jnp.arange(nb) * Q_BLOCK))
    out = out.transpose(1, 0, 2, 3, 4).reshape(bsz, seq, h, 2 * dh)
    out = rms_norm(out, subln_gain) * (1.0 - lambda_init)
    return out.reshape(bsz, seq, h * 2 * dh)


def setup_inputs(seed: int = 0) -> dict:
    key = jax.random.key(seed)
    ks = jax.random.split(key, 32)
    f32 = jnp.float32
    nrm = lambda k, shape, scale: jax.random.normal(k, shape, f32) * scale
    L, G, P = DEPTH, SSM_GROUPS, SSM_STATE
    return {
        "x": jax.random.normal(ks[0], (BATCH, SEQ, D_MODEL), f32),
        "norm_mix": 1.0 + nrm(ks[1], (L, D_MODEL), 0.02),
        "w_in": nrm(ks[2], (L, D_MODEL, IN_WIDTH), D_MODEL ** -0.5),
        "ssm_a_re": -0.5 + nrm(ks[3], (L, G, P), 0.01),
        "ssm_a_im": jnp.pi * jnp.arange(P, dtype=f32)[None, None, :] + nrm(ks[4], (L, G, P), 0.01),
        "ssm_log_dt": jax.random.uniform(ks[5], (L, G), f32, math.log(1e-3), math.log(1e-1)),
        "ssm_b_re": nrm(ks[6], (L, G, P, SSM_GROUP), (2 * SSM_GROUP) ** -0.5),
        "ssm_b_im": nrm(ks[7], (L, G, P, SSM_GROUP), (2 * SSM_GROUP) ** -0.5),
        "ssm_c_re": nrm(ks[8], (L, G, SSM_GROUP, P), (2 * P) ** -0.5),
        "ssm_c_im": nrm(ks[9], (L, G, SSM_GROUP, P), (2 * P) ** -0.5),
        "ssm_d": nrm(ks[10], (L, SSM_WIDTH), 1.0),
        "ssm_w_glu": nrm(ks[11], (L, SSM_WIDTH, SSM_WIDTH), SSM_WIDTH ** -0.5),
        "ssm_b_glu": nrm(ks[12], (L, SSM_WIDTH), 0.02),
        "ca_q_gain": 1.0 + nrm(ks[13], (L, HEAD_DIM), 0.02),
        "ca_k_gain": 1.0 + nrm(ks[14], (L, HEAD_DIM), 0.02),
        "ca_rel_bias": nrm(ks[15], (L, CA_HEADS, 2 * REL_CLIP + 1), 0.1),
        "da_q_gain": 1.0 + nrm(ks[16], (L, HEAD_DIM), 0.02),
        "da_k_gain": 1.0 + nrm(ks[17], (L, HEAD_DIM), 0.02),
        "da_lam_q1": nrm(ks[18], (L, HEAD_DIM), 0.1),
        "da_lam_k1": nrm(ks[19], (L, HEAD_DIM), 0.1),
        "da_lam_q2": nrm(ks[20], (L, HEAD_DIM), 0.1),
        "da_lam_k2": nrm(ks[21], (L, HEAD_DIM), 0.1),
        "da_subln_gain": 1.0 + nrm(ks[22], (L, 2 * HEAD_DIM), 0.02),
        "w_out_a": nrm(ks[23], (L, SSM_WIDTH, D_MODEL), SSM_WIDTH ** -0.5),
        "w_out_b": nrm(ks[24], (L, CA_WIDTH, D_MODEL), CA_WIDTH ** -0.5),
        "w_out_c": nrm(ks[25], (L, DA_WIDTH, D_MODEL), DA_WIDTH ** -0.5),
        "w_o": nrm(ks[26], (L, D_MODEL, D_MODEL), D_MODEL ** -0.5),
        "norm_mlp": 1.0 + nrm(ks[27], (L, D_MODEL), 0.02),
        "w_ff1": nrm(ks[28], (L, D_MODEL, D_FF), D_MODEL ** -0.5),
        "w_ff2": nrm(ks[29], (L, D_FF, D_MODEL), D_FF ** -0.5),
    }


def reference(x, norm_mix, w_in, ssm_a_re, ssm_a_im, ssm_log_dt, ssm_b_re, ssm_b_im, ssm_c_re, ssm_c_im,
              ssm_d, ssm_w_glu, ssm_b_glu, ca_q_gain, ca_k_gain, ca_rel_bias, da_q_gain, da_k_gain,
              da_lam_q1, da_lam_k1, da_lam_q2, da_lam_k2, da_subln_gain, w_out_a, w_out_b, w_out_c,
              w_o, norm_mlp, w_ff1, w_ff2):
    bsz, seq, _ = x.shape
    widths = [SSM_WIDTH, CA_WIDTH, CA_WIDTH, CA_WIDTH, DA_WIDTH, DA_WIDTH, DA_WIDTH]
    split_points = [int(v) for v in np.cumsum(widths)]
    for l in range(DEPTH):
        lambda_init = 0.8 - 0.6 * math.exp(-0.3 * l)
        h = rms_norm(x, norm_mix[l])
        proj = h @ w_in[l]
        u_a, q_b, k_b, v_b, q_c, k_c, v_c, gates = jnp.split(proj, split_points, axis=-1)
        y_a = s5_mixer(u_a, ssm_a_re[l], ssm_a_im[l], ssm_log_dt[l], ssm_b_re[l], ssm_b_im[l],
                       ssm_c_re[l], ssm_c_im[l], ssm_d[l], ssm_w_glu[l], ssm_b_glu[l])
        y_b = chunk_band_attention(q_b.reshape(bsz, seq, CA_HEADS, HEAD_DIM),
                                   k_b.reshape(bsz, seq, CA_HEADS, HEAD_DIM),
                                   v_b.reshape(bsz, seq, CA_HEADS, HEAD_DIM),
                                   ca_q_gain[l], ca_k_gain[l], ca_rel_bias[l])
        y_c = diff_attention(q_c.reshape(bsz, seq, DA_HEADS, 2, HEAD_DIM),
                             k_c.reshape(bsz, seq, DA_HEADS, 2, HEAD_DIM),
                             v_c.reshape(bsz, seq, DA_HEADS, 2 * HEAD_DIM),
                             da_q_gain[l], da_k_gain[l], da_lam_q1[l], da_lam_k1[l],
                             da_lam_q2[l], da_lam_k2[l], da_subln_gain[l], lambda_init)
        g = jax.nn.sigmoid(gates.reshape(bsz, seq, N_BRANCH, D_MODEL))
        merged = (g[:, :, 0] * (y_a @ w_out_a[l])
                  + g[:, :, 1] * (y_b @ w_out_b[l])
                  + g[:, :, 2] * (y_c @ w_out_c[l]))
        x = x + merged @ w_o[l]
        h = rms_norm(x, norm_mlp[l])
        x = x + jnp.square(jax.nn.relu(h @ w_ff1[l])) @ w_ff2[l]
    return x
```

```python
import functools
import math

import jax
import jax.numpy as jnp
from jax import lax
from jax.experimental import pallas as pl
from jax.experimental.pallas import tpu as pltpu

F32 = jnp.float32
BF16 = jnp.bfloat16

EPS = 1e-6
HEAD_DIM = 128
CHUNK = 64
SSM_GROUP = 16
SSM_CHUNK = 16
CA_LEFT_CHUNKS = 8
REL_CLIP = 128
MASK_VALUE = -1e30

V7X_VMEM_BYTES = 64 * 1024 * 1024
VMEM_LIMIT = V7X_VMEM_BYTES - 8 * 1024 * 1024


def _params(*semantics):
    return pltpu.CompilerParams(dimension_semantics=semantics, vmem_limit_bytes=VMEM_LIMIT)


def _rms(x, gain):
    return x * lax.rsqrt(jnp.mean(x * x, axis=-1, keepdims=True) + EPS) * gain


def _sigmoid(x):
    return 1.0 / (1.0 + jnp.exp(-x))


def _tile(dim, preferred, align=128):
    best = None
    for t in range(align, min(dim, preferred) + 1, align):
        if dim % t == 0:
            best = t
    assert best is not None, (dim, preferred)
    return best


def _rmsnorm_kernel(x_ref, g_ref, o_ref):
    o_ref[...] = _rms(x_ref[...], g_ref[...]).astype(o_ref.dtype)


def rmsnorm(x, gain, *, tm=256):
    t, d = x.shape
    tm = _tile(t, tm, align=8)
    return pl.pallas_call(
        _rmsnorm_kernel,
        grid=(t // tm,),
        in_specs=[pl.BlockSpec((tm, d), lambda i: (i, 0)),
                  pl.BlockSpec((1, d), lambda i: (0, 0))],
        out_specs=pl.BlockSpec((tm, d), lambda i: (i, 0)),
        out_shape=jax.ShapeDtypeStruct((t, d), BF16),
        compiler_params=_params("parallel"),
    )(x, gain.reshape(1, d))


def _matmul_kernel(*refs, epilogue, nk):
    if epilogue == "residual":
        a_ref, w_ref, r_ref, o_ref = refs
    else:
        a_ref, w_ref, o_ref = refs
    acc = jnp.dot(a_ref[...], w_ref[...], preferred_element_type=F32)
    if nk == 1:
        if epilogue == "residual":
            acc = acc + r_ref[...]
        elif epilogue == "relu2":
            acc = jnp.square(jnp.maximum(acc, 0.0))
        o_ref[...] = acc.astype(o_ref.dtype)
    else:
        k = pl.program_id(2)

        @pl.when(k == 0)
        def _():
            o_ref[...] = acc + r_ref[...]

        @pl.when(k > 0)
        def _():
            o_ref[...] += acc


def matmul(a, w, *, out_dtype, epilogue="none", residual=None, tm=1024, tn=1024, tk=None):
    m, kdim = a.shape
    _, n = w.shape
    tm, tn = _tile(m, tm), _tile(n, tn)
    tk = kdim if tk is None else _tile(kdim, tk)
    nk = kdim // tk
    if nk > 1:
        assert epilogue == "residual" and out_dtype == F32
    in_specs = [pl.BlockSpec((tm, tk), lambda j, i, k: (i, k)),
                pl.BlockSpec((tk, tn), lambda j, i, k: (k, j))]
    args = [a, w]
    if epilogue == "residual":
        in_specs.append(pl.BlockSpec((tm, tn), lambda j, i, k: (i, j)))
        args.append(residual)
    return pl.pallas_call(
        functools.partial(_matmul_kernel, epilogue=epilogue, nk=nk),
        grid=(n // tn, m // tm, nk),
        in_specs=in_specs,
        out_specs=pl.BlockSpec((tm, tn), lambda j, i, k: (i, j)),
        out_shape=jax.ShapeDtypeStruct((m, n), out_dtype),
        compiler_params=_params("parallel", "parallel", "arbitrary"),
    )(*args)


def _s5_matrices(a_re, a_im, log_dt, b_re, b_im, c_re, c_im, n_chunks):
    L = SSM_CHUNK
    g, p = a_re.shape
    h = b_re.shape[-1]
    hi = lax.Precision.HIGHEST
    lam = lax.complex(jnp.minimum(a_re.astype(F32), -1e-4), a_im.astype(F32))
    z = lam * jnp.exp(log_dt.astype(F32))[:, None]
    a_bar = jnp.exp(z)
    b_bar = ((a_bar - 1.0) / lam)[..., None] * lax.complex(b_re.astype(F32), b_im.astype(F32))
    c = lax.complex(c_re.astype(F32), c_im.astype(F32))
    apow = jnp.exp(z[..., None] * jnp.arange(L + 1, dtype=F32))
    kern = jnp.einsum("gip,gpt,gpj->gtij", c, apow[..., :L], b_bar, precision=hi).real
    step = jnp.arange(L)
    lag = step[None, :] - step[:, None]
    toep = jnp.where((lag >= 0)[None, :, :, None, None], kern[:, jnp.maximum(lag, 0)], 0.0)
    toep = toep.transpose(0, 1, 4, 2, 3).reshape(g, L * h, L * h)
    ws = jnp.einsum("gps,gpj->gsjp", apow[..., :L][..., ::-1], b_bar, precision=hi)
    ws = ws.reshape(g, L * h, p)
    wsum = jnp.concatenate([ws.real, ws.imag], axis=-1)
    wc = jnp.einsum("gip,gpt->gpti", c, apow[..., 1:], precision=hi).reshape(g, p, L * h)
    wcarry = jnp.concatenate([wc.real, -wc.imag], axis=1)
    n_steps = int(math.log2(n_chunks))
    ak = jnp.exp(z[:, None, :] * (L * 2.0 ** jnp.arange(n_steps, dtype=F32))[None, :, None])
    ar = jnp.concatenate([ak.real, ak.real], axis=-1)
    ai = jnp.concatenate([-ak.imag, ak.imag], axis=-1)
    return toep.astype(BF16), wsum.astype(BF16), wcarry.astype(BF16), ar, ai


def _s5_kernel(u_ref, toep_ref, wsum_ref, wcarry_ref, ar_ref, ai_ref, y_ref, *, n_chunks):
    u = u_ref[0]
    y = jnp.dot(u, toep_ref[0], preferred_element_type=F32)
    state = jnp.dot(u, wsum_ref[0], preferred_element_type=F32)
    half = state.shape[-1] // 2
    chunk = lax.broadcasted_iota(jnp.int32, state.shape, 0) % n_chunks
    for k in range(ar_ref.shape[1]):
        shift = 1 << k
        prev = jnp.where(chunk >= shift, pltpu.roll(state, shift, 0), 0.0)
        state = (state + ar_ref[0, k:k + 1, :] * prev
                 + ai_ref[0, k:k + 1, :] * pltpu.roll(prev, half, 1))
    incoming = jnp.where(chunk >= 1, pltpu.roll(state, 1, 0), 0.0)
    y = y + jnp.dot(incoming.astype(BF16), wcarry_ref[0], preferred_element_type=F32)
    y_ref[0] = y


def s5_scan(u_g, mats, n_chunks):
    g, rows, width = u_g.shape
    toep, wsum, wcarry, ar, ai = mats
    blk = lambda a: pl.BlockSpec((1,) + a.shape[1:], lambda i: (i, 0, 0))
    return pl.pallas_call(
        functools.partial(_s5_kernel, n_chunks=n_chunks),
        grid=(g,),
        in_specs=[blk(u_g), blk(toep), blk(wsum), blk(wcarry), blk(ar), blk(ai)],
        out_specs=pl.BlockSpec((1, rows, width), lambda i: (i, 0, 0)),
        out_shape=jax.ShapeDtypeStruct((g, rows, width), F32),
        compiler_params=_params("parallel"),
    )(u_g, toep, wsum, wcarry, ar, ai)


def _s5_glu_kernel(y_ref, u_ref, d_ref, w_ref, b_ref, o_ref):
    y = y_ref[...] + d_ref[...] * u_ref[...].astype(F32)
    z = y * (0.5 * (1.0 + jnp.tanh(math.sqrt(2.0 / math.pi) * (y + 0.044715 * (y * y * y)))))
    gate = jnp.dot(z.astype(BF16), w_ref[...], preferred_element_type=F32) + b_ref[...]
    o_ref[...] = (z * _sigmoid(gate)).astype(o_ref.dtype)


def s5_glu(y, proj, d_skip, w_glu, b_glu, *, tm=512):
    t, w = y.shape
    tm = _tile(t, tm, align=8)
    row = lambda a: pl.BlockSpec((1, w), lambda i: (0, 0))
    return pl.pallas_call(
        _s5_glu_kernel,
        grid=(t // tm,),
        in_specs=[pl.BlockSpec((tm, w), lambda i: (i, 0)),
                  pl.BlockSpec((tm, w), lambda i: (i, 0)),
                  row(d_skip),
                  pl.BlockSpec((w, w), lambda i: (0, 0)),
                  row(b_glu)],
        out_specs=pl.BlockSpec((tm, w), lambda i: (i, 0)),
        out_shape=jax.ShapeDtypeStruct((t, w), BF16),
        compiler_params=_params("parallel"),
    )(y, proj, d_skip.reshape(1, w), w_glu, b_glu.reshape(1, w))


CA_QBLOCK = 2 * CHUNK
CA_WINDOW = CA_LEFT_CHUNKS * CHUNK + CA_QBLOCK
CA_LEAD = CA_LEFT_CHUNKS * CHUNK // CA_QBLOCK


def _ca_bias(rel_bias):
    o = jnp.arange(CA_LEAD + 1)[:, None, None]
    q_pos = o * CA_QBLOCK + jnp.arange(CA_QBLOCK)[None, :, None]
    k_pos = jnp.arange(CA_WINDOW)[None, None, :]
    q_chunk, k_chunk = q_pos // CHUNK, k_pos // CHUNK
    allowed = (k_chunk <= q_chunk) & (k_chunk >= q_chunk - CA_LEFT_CHUNKS)
    idx = jnp.clip(q_pos - k_pos, -REL_CLIP, REL_CLIP) + REL_CLIP
    bias = rel_bias.astype(F32)[:, idx]
    return jnp.where(allowed[None], bias, MASK_VALUE).transpose(1, 0, 2, 3)


def _ca_kernel(q_ref, k_ref, v_ref, bias_ref, qg_ref, kg_ref, o_ref, kn_ref, *, scale):
    kn_ref[...] = _rms(k_ref[...].astype(F32), kg_ref[...]).astype(BF16)
    n_blocks = q_ref.shape[0] // CA_QBLOCK

    def body(i, carry):
        r0 = pl.multiple_of(i * CA_QBLOCK, CA_QBLOCK)
        w0 = pl.multiple_of(jnp.maximum(i - CA_LEAD, 0) * CA_QBLOCK, CA_QBLOCK)
        q = (_rms(q_ref[pl.ds(r0, CA_QBLOCK), :].astype(F32), qg_ref[...]) * scale).astype(BF16)
        s = lax.dot_general(q, kn_ref[pl.ds(w0, CA_WINDOW), :], (((1,), (1,)), ((), ())),
                            preferred_element_type=F32)
        s = s + bias_ref[jnp.minimum(i, CA_LEAD), 0]
        p = jnp.exp(s - jnp.max(s, axis=-1, keepdims=True))
        denom = jnp.sum(p, axis=-1, keepdims=True)
        out = jnp.dot(p.astype(BF16), v_ref[pl.ds(w0, CA_WINDOW), :], preferred_element_type=F32)
        o_ref[pl.ds(r0, CA_QBLOCK), :] = (out / denom).astype(o_ref.dtype)
        return carry

    lax.fori_loop(0, n_blocks, body, 0)


def chunk_band_attention(proj, q_col, k_col, v_col, heads, bsz, seq, q_gain, k_gain, rel_bias):
    bias = _ca_bias(rel_bias)
    head_block = lambda col: pl.BlockSpec((seq, HEAD_DIM), lambda h, b: (b, col + h))
    gain = pl.BlockSpec((1, HEAD_DIM), lambda h, b: (0, 0))
    return pl.pallas_call(
        functools.partial(_ca_kernel, scale=HEAD_DIM ** -0.5),
        grid=(heads, bsz),
        in_specs=[head_block(q_col), head_block(k_col), head_block(v_col),
                  pl.BlockSpec((CA_LEAD + 1, 1, CA_QBLOCK, CA_WINDOW), lambda h, b: (0, h, 0, 0)),
                  gain, gain],
        out_specs=pl.BlockSpec((seq, HEAD_DIM), lambda h, b: (b, h)),
        out_shape=jax.ShapeDtypeStruct((bsz * seq, heads * HEAD_DIM), BF16),
        scratch_shapes=[pltpu.VMEM((seq, HEAD_DIM), BF16)],
        compiler_params=_params("parallel", "parallel"),
    )(proj, proj, proj, bias, q_gain.reshape(1, HEAD_DIM), k_gain.reshape(1, HEAD_DIM))


DA_BLOCK = 256


def _da_kernel(q1_ref, q2_ref, k1_ref, k2_ref, v_ref, qg_ref, kg_ref, lam_ref, slope_ref, sg_ref,
               o_ref, kn_ref, *, scale, lambda_init):
    tq = DA_BLOCK
    n_blocks = q1_ref.shape[0] // tq
    for c, k_ref in enumerate((k1_ref, k2_ref)):
        kn_ref[c] = _rms(k_ref[...].astype(F32), kg_ref[...]).astype(BF16)
    lam = (jnp.exp(jnp.sum(lam_ref[0:1, :] * lam_ref[1:2, :], axis=-1, keepdims=True))
           - jnp.exp(jnp.sum(lam_ref[2:3, :] * lam_ref[3:4, :], axis=-1, keepdims=True))
           + lambda_init)
    slope = slope_ref[0, :, 0:1]
    row = lax.broadcasted_iota(jnp.int32, (tq, tq), 0)
    col = lax.broadcasted_iota(jnp.int32, (tq, tq), 1)
    delta = (row - col).astype(F32)
    alibi_past = slope * delta
    alibi_diag = slope * jnp.abs(delta)
    allowed = (col // CHUNK) <= (row // CHUNK)

    def update(carry, s, v):
        m, l, acc = carry
        m_new = jnp.maximum(m, jnp.max(s, axis=-1, keepdims=True))
        alpha = jnp.exp(m - m_new)
        p = jnp.exp(s - m_new)
        l = alpha * l + jnp.sum(p, axis=-1, keepdims=True)
        acc = alpha * acc + jnp.dot(p.astype(BF16), v, preferred_element_type=F32)
        return m_new, l, acc

    def scores(q, c, c0):
        return lax.dot_general(q, kn_ref[c, pl.ds(c0, tq), :], (((1,), (1,)), ((), ())),
                               preferred_element_type=F32)

    def q_block(i, carry):
        r0 = pl.multiple_of(i * tq, tq)
        outs = []
        for c, q_ref in enumerate((q1_ref, q2_ref)):
            q = (_rms(q_ref[pl.ds(r0, tq), :].astype(F32), qg_ref[...]) * scale).astype(BF16)

            def k_block(j, st):
                c0 = pl.multiple_of(j * tq, tq)
                gap = (jnp.zeros((1, 1), jnp.int32) + (i - j) * tq).astype(F32)
                s = scores(q, c, c0) - alibi_past - slope * gap
                return update(st, s, v_ref[pl.ds(c0, tq), :])

            init = (jnp.full((tq, 1), MASK_VALUE, F32), jnp.zeros((tq, 1), F32),
                    jnp.zeros((tq, v_ref.shape[1]), F32))
            st = lax.fori_loop(0, i, k_block, init)
            s = jnp.where(allowed, scores(q, c, r0) - alibi_diag, MASK_VALUE)
            _, l, acc = update(st, s, v_ref[pl.ds(r0, tq), :])
            outs.append(acc / l)
        out = outs[0] - lam * outs[1]
        out = _rms(out, sg_ref[...]) * (1.0 - lambda_init)
        o_ref[pl.ds(r0, tq), :] = out.astype(o_ref.dtype)
        return carry

    lax.fori_loop(0, n_blocks, q_block, 0)


def diff_attention(proj, q_col, k_col, v_col, heads, bsz, seq, q_gain, k_gain, lam_vecs, subln_gain,
                   lambda_init):
    slopes = 2.0 ** (-8.0 * jnp.arange(1, heads + 1, dtype=F32) / heads)
    slopes = jnp.broadcast_to(slopes[:, None, None], (heads, 1, HEAD_DIM))
    qk_block = lambda col: pl.BlockSpec((seq, HEAD_DIM), lambda b, h: (b, col + 2 * h))
    vec = lambda n: pl.BlockSpec((1, n), lambda b, h: (0, 0))
    return pl.pallas_call(
        functools.partial(_da_kernel, scale=HEAD_DIM ** -0.5, lambda_init=lambda_init),
        grid=(bsz, heads),
        in_specs=[qk_block(q_col), qk_block(q_col + 1), qk_block(k_col), qk_block(k_col + 1),
                  pl.BlockSpec((seq, 2 * HEAD_DIM), lambda b, h: (b, v_col + h)),
                  vec(HEAD_DIM), vec(HEAD_DIM),
                  pl.BlockSpec((4, HEAD_DIM), lambda b, h: (0, 0)),
                  pl.BlockSpec((1, 1, HEAD_DIM), lambda b, h: (h, 0, 0)),
                  vec(2 * HEAD_DIM)],
        out_specs=pl.BlockSpec((seq, 2 * HEAD_DIM), lambda b, h: (b, h)),
        out_shape=jax.ShapeDtypeStruct((bsz * seq, heads * 2 * HEAD_DIM), BF16),
        scratch_shapes=[pltpu.VMEM((2, seq, HEAD_DIM), BF16)],
        compiler_params=_params("parallel", "parallel"),
    )(proj, proj, proj, proj, proj, q_gain.reshape(1, HEAD_DIM), k_gain.reshape(1, HEAD_DIM),
      lam_vecs, slopes, subln_gain.reshape(1, 2 * HEAD_DIM))


def _merge_kernel(ya_ref, yb_ref, yc_ref, wa_ref, wb_ref, wc_ref, ga_ref, gb_ref, gc_ref, o_ref):
    def branch(y_ref, w_ref, g_ref):
        out = jnp.dot(y_ref[...], w_ref[...], preferred_element_type=F32)
        return _sigmoid(g_ref[...].astype(F32)) * out

    merged = (branch(ya_ref, wa_ref, ga_ref) + branch(yb_ref, wb_ref, gb_ref)
              + branch(yc_ref, wc_ref, gc_ref))
    o_ref[...] = merged.astype(o_ref.dtype)


def gated_merge(ys, ws, proj, gate_col, *, tm=512, tn=512):
    t = ys[0].shape[0]
    d = ws[0].shape[1]
    tm, tn = _tile(t, tm, align=8), _tile(math.gcd(d, gate_col), tn)
    y_spec = lambda y: pl.BlockSpec((tm, y.shape[1]), lambda j, i: (i, 0))
    w_spec = lambda w: pl.BlockSpec((w.shape[0], tn), lambda j, i: (0, j))
    g_spec = lambda b: pl.BlockSpec((tm, tn), lambda j, i: (i, (gate_col + b * d) // tn + j))
    return pl.pallas_call(
        _merge_kernel,
        grid=(d // tn, t // tm),
        in_specs=[y_spec(y) for y in ys] + [w_spec(w) for w in ws] + [g_spec(b) for b in range(3)],
        out_specs=pl.BlockSpec((tm, tn), lambda j, i: (i, j)),
        out_shape=jax.ShapeDtypeStruct((t, d), BF16),
        compiler_params=_params("parallel", "parallel"),
    )(*ys, *ws, proj, proj, proj)


def kernel(x, norm_mix, w_in, ssm_a_re, ssm_a_im, ssm_log_dt, ssm_b_re, ssm_b_im, ssm_c_re, ssm_c_im, ssm_d, ssm_w_glu, ssm_b_glu, ca_q_gain, ca_k_gain, ca_rel_bias, da_q_gain, da_k_gain, da_lam_q1, da_lam_k1, da_lam_q2, da_lam_k2, da_subln_gain, w_out_a, w_out_b, w_out_c, w_o, norm_mlp, w_ff1, w_ff2):
    bsz, seq, d_model = x.shape
    depth = w_in.shape[0]
    ssm_w, ca_w, da_w = w_out_a.shape[1], w_out_b.shape[1], w_out_c.shape[1]
    groups = ssm_w // SSM_GROUP
    ca_heads, da_heads = ca_w // HEAD_DIM, da_w // (2 * HEAD_DIM)
    n_chunks = seq // SSM_CHUNK
    ca_q = ssm_w // HEAD_DIM
    ca_k, ca_v = ca_q + ca_heads, ca_q + 2 * ca_heads
    da_q = ca_q + 3 * ca_heads
    da_k, da_v = da_q + 2 * da_heads, da_q + 4 * da_heads
    gate_col = ssm_w + 3 * ca_w + 3 * da_w

    xt = x.reshape(bsz * seq, d_model)
    for l in range(depth):
        lambda_init = 0.8 - 0.6 * math.exp(-0.3 * l)
        h = rmsnorm(xt, norm_mix[l])
        proj = matmul(h, w_in[l].astype(BF16), out_dtype=BF16)

        u_g = proj[:, :ssm_w].reshape(bsz, n_chunks, SSM_CHUNK, groups, SSM_GROUP)
        u_g = u_g.transpose(3, 0, 1, 2, 4).reshape(groups, bsz * n_chunks, SSM_CHUNK * SSM_GROUP)
        mats = _s5_matrices(ssm_a_re[l], ssm_a_im[l], ssm_log_dt[l], ssm_b_re[l], ssm_b_im[l],
                            ssm_c_re[l], ssm_c_im[l], n_chunks)
        y_g = s5_scan(u_g, mats, n_chunks)
        y_s5 = y_g.reshape(groups, bsz, n_chunks, SSM_CHUNK, SSM_GROUP)
        y_s5 = y_s5.transpose(1, 2, 3, 0, 4).reshape(bsz * seq, ssm_w)
        y_a = s5_glu(y_s5, proj, ssm_d[l], ssm_w_glu[l].astype(BF16), ssm_b_glu[l])

        y_b = chunk_band_attention(proj, ca_q, ca_k, ca_v, ca_heads, bsz, seq,
                                   ca_q_gain[l], ca_k_gain[l], ca_rel_bias[l])
        lam_vecs = jnp.stack([da_lam_q1[l], da_lam_k1[l], da_lam_q2[l], da_lam_k2[l]]).astype(F32)
        y_c = diff_attention(proj, da_q, da_k, da_v // 2, da_heads, bsz, seq,
                             da_q_gain[l], da_k_gain[l], lam_vecs, da_subln_gain[l], lambda_init)

        merged = gated_merge((y_a, y_b, y_c),
                             (w_out_a[l].astype(BF16), w_out_b[l].astype(BF16),
                              w_out_c[l].astype(BF16)), proj, gate_col)
        xt = matmul(merged, w_o[l].astype(BF16), out_dtype=F32, epilogue="residual", residual=xt,
                    tn=512)
        h = rmsnorm(xt, norm_mlp[l])
        ff = matmul(h, w_ff1[l].astype(BF16), out_dtype=BF16, epilogue="relu2")
        xt = matmul(ff, w_ff2[l].astype(BF16), out_dtype=F32, epilogue="residual", residual=xt,
                    tn=512, tk=4096)
    return xt.reshape(bsz, seq, d_model)
```

```python
import functools
import math

import jax
import jax.numpy as jnp
from jax import lax
from jax.experimental import pallas as pl
from jax.experimental.pallas import tpu as pltpu

F32 = jnp.float32
BF16 = jnp.bfloat16

EPS = 1e-6
HEAD_DIM = 128
CHUNK = 64
SSM_GROUP = 16
SSM_CHUNK = 16
CA_LEFT_CHUNKS = 8
REL_CLIP = 128
MASK_VALUE = -1e30

V7X_VMEM_BYTES = 64 * 1024 * 1024
VMEM_LIMIT = V7X_VMEM_BYTES - 8 * 1024 * 1024


def _params(*semantics):
    return pltpu.CompilerParams(dimension_semantics=semantics, vmem_limit_bytes=VMEM_LIMIT)


def _rms(x, gain):
    return x * lax.rsqrt(jnp.mean(x * x, axis=-1, keepdims=True) + EPS) * gain


def _sigmoid(x):
    return 1.0 / (1.0 + jnp.exp(-x))


def _tile(dim, preferred, align=128):
    best = None
    for t in range(align, min(dim, preferred) + 1, align):
        if dim % t == 0:
            best = t
    assert best is not None, (dim, preferred)
    return best


def _rmsnorm_kernel(x_ref, g_ref, o_ref):
    o_ref[...] = _rms(x_ref[...], g_ref[...]).astype(o_ref.dtype)


def rmsnorm(x, gain, *, tm=256):
    t, d = x.shape
    tm = _tile(t, tm, align=8)
    return pl.pallas_call(
        _rmsnorm_kernel,
        grid=(t // tm,),
        in_specs=[pl.BlockSpec((tm, d), lambda i: (i, 0)),
                  pl.BlockSpec((1, d), lambda i: (0, 0))],
        out_specs=pl.BlockSpec((tm, d), lambda i: (i, 0)),
        out_shape=jax.ShapeDtypeStruct((t, d), BF16),
        compiler_params=_params("parallel"),
    )(x, gain.reshape(1, d))


def _matmul_kernel(*refs, epilogue, nk):
    if epilogue == "residual":
        a_ref, w_ref, r_ref, o_ref = refs
    else:
        a_ref, w_ref, o_ref = refs
    acc = jnp.dot(a_ref[...], w_ref[...], preferred_element_type=F32)
    if nk == 1:
        if epilogue == "residual":
            acc = acc + r_ref[...]
        elif epilogue == "relu2":
            acc = jnp.square(jnp.maximum(acc, 0.0))
        o_ref[...] = acc.astype(o_ref.dtype)
    else:
        k = pl.program_id(2)

        @pl.when(k == 0)
        def _():
            o_ref[...] = acc + r_ref[...]

        @pl.when(k > 0)
        def _():
            o_ref[...] += acc


def matmul(a, w, *, out_dtype, epilogue="none", residual=None, tm=1024, tn=1024, tk=None):
    m, kdim = a.shape
    _, n = w.shape
    tm, tn = _tile(m, tm), _tile(n, tn)
    tk = kdim if tk is None else _tile(kdim, tk)
    nk = kdim // tk
    if nk > 1:
        assert epilogue == "residual" and out_dtype == F32
    in_specs = [pl.BlockSpec((tm, tk), lambda j, i, k: (i, k)),
                pl.BlockSpec((tk, tn), lambda j, i, k: (k, j))]
    args = [a, w]
    if epilogue == "residual":
        in_specs.append(pl.BlockSpec((tm, tn), lambda j, i, k: (i, j)))
        args.append(residual)
    return pl.pallas_call(
        functools.partial(_matmul_kernel, epilogue=epilogue, nk=nk),
        grid=(n // tn, m // tm, nk),
        in_specs=in_specs,
        out_specs=pl.BlockSpec((tm, tn), lambda j, i, k: (i, j)),
        out_shape=jax.ShapeDtypeStruct((m, n), out_dtype),
        compiler_params=_params("parallel", "parallel", "arbitrary"),
    )(*args)


def _s5_matrices(a_re, a_im, log_dt, b_re, b_im, c_re, c_im, n_chunks):
    L = SSM_CHUNK
    g, p = a_re.shape
    h = b_re.shape[-1]
    hi = lax.Precision.HIGHEST

    def cmul(xr, xi, yr, yi):
        return xr * yr - xi * yi, xr * yi + xi * yr

    lam_r, lam_i = jnp.minimum(a_re.astype(F32), -1e-4), a_im.astype(F32)
    dt = jnp.exp(log_dt.astype(F32))[:, None]
    z_r, z_i = lam_r * dt, lam_i * dt

    def a_pow(n):
        mag, ang = jnp.exp(z_r[..., None] * n), z_i[..., None] * n
        return mag * jnp.cos(ang), mag * jnp.sin(ang)

    ab_r, ab_i = jnp.exp(z_r) * jnp.cos(z_i), jnp.exp(z_r) * jnp.sin(z_i)
    den = lam_r * lam_r + lam_i * lam_i
    f_r = ((ab_r - 1.0) * lam_r + ab_i * lam_i) / den
    f_i = (ab_i * lam_r - (ab_r - 1.0) * lam_i) / den
    bb_r, bb_i = cmul(f_r[..., None], f_i[..., None], b_re.astype(F32), b_im.astype(F32))
    pw_r, pw_i = a_pow(jnp.arange(L + 1, dtype=F32))
    cr, ci = c_re.astype(F32)[..., None], c_im.astype(F32)[..., None]

    ca_r, ca_i = cmul(cr, ci, pw_r[:, None, :, :L], pw_i[:, None, :, :L])
    kern = (jnp.einsum("gipt,gpj->gtij", ca_r, bb_r, precision=hi)
            - jnp.einsum("gipt,gpj->gtij", ca_i, bb_i, precision=hi))
    step = jnp.arange(L)
    lag = step[None, :] - step[:, None]
    toep = jnp.where((lag >= 0)[None, :, :, None, None], kern[:, jnp.maximum(lag, 0)], 0.0)
    toep = toep.transpose(0, 1, 4, 2, 3).reshape(g, L * h, L * h)

    ws_r, ws_i = cmul(pw_r[..., L - 1::-1][..., None], pw_i[..., L - 1::-1][..., None],
                      bb_r[:, :, None, :], bb_i[:, :, None, :])
    to_rows = lambda a: a.transpose(0, 2, 3, 1).reshape(g, L * h, p)
    wsum = jnp.concatenate([to_rows(ws_r), to_rows(ws_i)], axis=-1)

    wc_r, wc_i = cmul(cr, ci, pw_r[:, None, :, 1:], pw_i[:, None, :, 1:])
    to_cols = lambda a: a.transpose(0, 2, 3, 1).reshape(g, p, L * h)
    wcarry = jnp.concatenate([to_cols(wc_r), -to_cols(wc_i)], axis=1)

    n_steps = int(math.log2(n_chunks))
    ak_r, ak_i = a_pow(L * 2.0 ** jnp.arange(n_steps, dtype=F32))
    ak_r, ak_i = ak_r.transpose(0, 2, 1), ak_i.transpose(0, 2, 1)
    ar = jnp.concatenate([ak_r, ak_r], axis=-1)
    ai = jnp.concatenate([-ak_i, ak_i], axis=-1)
    return toep.astype(BF16), wsum.astype(BF16), wcarry.astype(BF16), ar, ai


def _s5_kernel(u_ref, toep_ref, wsum_ref, wcarry_ref, ar_ref, ai_ref, y_ref, *, n_chunks):
    u = u_ref[0]
    y = jnp.dot(u, toep_ref[0], preferred_element_type=F32)
    state = jnp.dot(u, wsum_ref[0], preferred_element_type=F32)
    half = state.shape[-1] // 2
    chunk = lax.broadcasted_iota(jnp.int32, state.shape, 0) % n_chunks
    for k in range(ar_ref.shape[1]):
        shift = 1 << k
        prev = jnp.where(chunk >= shift, pltpu.roll(state, shift, 0), 0.0)
        state = (state + ar_ref[0, k:k + 1, :] * prev
                 + ai_ref[0, k:k + 1, :] * pltpu.roll(prev, half, 1))
    incoming = jnp.where(chunk >= 1, pltpu.roll(state, 1, 0), 0.0)
    y = y + jnp.dot(incoming.astype(BF16), wcarry_ref[0], preferred_element_type=F32)
    y_ref[0] = y


def s5_scan(u_g, mats, n_chunks):
    g, rows, width = u_g.shape
    toep, wsum, wcarry, ar, ai = mats
    blk = lambda a: pl.BlockSpec((1,) + a.shape[1:], lambda i: (i, 0, 0))
    return pl.pallas_call(
        functools.partial(_s5_kernel, n_chunks=n_chunks),
        grid=(g,),
        in_specs=[blk(u_g), blk(toep), blk(wsum), blk(wcarry), blk(ar), blk(ai)],
        out_specs=pl.BlockSpec((1, rows, width), lambda i: (i, 0, 0)),
        out_shape=jax.ShapeDtypeStruct((g, rows, width), F32),
        compiler_params=_params("parallel"),
    )(u_g, toep, wsum, wcarry, ar, ai)


def _s5_glu_kernel(y_ref, u_ref, d_ref, w_ref, b_ref, o_ref):
    y = y_ref[...] + d_ref[...] * u_ref[...].astype(F32)
    z = y * (0.5 * (1.0 + jnp.tanh(math.sqrt(2.0 / math.pi) * (y + 0.044715 * (y * y * y)))))
    gate = jnp.dot(z.astype(BF16), w_ref[...], preferred_element_type=F32) + b_ref[...]
    o_ref[...] = (z * _sigmoid(gate)).astype(o_ref.dtype)


def s5_glu(y, proj, d_skip, w_glu, b_glu, *, tm=512):
    t, w = y.shape
    tm = _tile(t, tm, align=8)
    row = lambda a: pl.BlockSpec((1, w), lambda i: (0, 0))
    return pl.pallas_call(
        _s5_glu_kernel,
        grid=(t // tm,),
        in_specs=[pl.BlockSpec((tm, w), lambda i: (i, 0)),
                  pl.BlockSpec((tm, w), lambda i: (i, 0)),
                  row(d_skip),
                  pl.BlockSpec((w, w), lambda i: (0, 0)),
                  row(b_glu)],
        out_specs=pl.BlockSpec((tm, w), lambda i: (i, 0)),
        out_shape=jax.ShapeDtypeStruct((t, w), BF16),
        compiler_params=_params("parallel"),
    )(y, proj, d_skip.reshape(1, w), w_glu, b_glu.reshape(1, w))


CA_QBLOCK = 2 * CHUNK
CA_WINDOW = CA_LEFT_CHUNKS * CHUNK + CA_QBLOCK
CA_LEAD = CA_LEFT_CHUNKS * CHUNK // CA_QBLOCK


def _ca_bias(rel_bias):
    heads = rel_bias.shape[0]
    o = jnp.arange(CA_LEAD + 1)[:, None, None]
    q_pos = o * CA_QBLOCK + jnp.arange(CA_QBLOCK)[None, :, None]
    k_pos = jnp.arange(CA_WINDOW)[None, None, :]
    q_chunk, k_chunk = q_pos // CHUNK, k_pos // CHUNK
    allowed = (k_chunk <= q_chunk) & (k_chunk >= q_chunk - CA_LEFT_CHUNKS)
    period = CA_WINDOW + CA_QBLOCK
    n = jnp.arange(period)
    k_minus_q = jnp.where(n < CA_WINDOW, n, n - period)
    idx = jnp.clip(o[:, :, 0] * CA_QBLOCK - k_minus_q[None, :], -REL_CLIP, REL_CLIP) + REL_CLIP
    line = rel_bias.astype(F32)[:, idx]
    tiled = jnp.tile(line, (1, 1, CA_QBLOCK))[..., :CA_QBLOCK * (period - 1)]
    bias = tiled.reshape(heads, CA_LEAD + 1, CA_QBLOCK, period - 1)[..., :CA_WINDOW]
    return jnp.where(allowed[None], bias, MASK_VALUE).transpose(1, 0, 2, 3)


def _ca_kernel(q_ref, k_ref, v_ref, bias_ref, qg_ref, kg_ref, o_ref, kn_ref, *, scale):
    kn_ref[...] = _rms(k_ref[...].astype(F32), kg_ref[...]).astype(BF16)
    n_blocks = q_ref.shape[0] // CA_QBLOCK

    def body(i, carry):
        r0 = pl.multiple_of(i * CA_QBLOCK, CA_QBLOCK)
        w0 = pl.multiple_of(jnp.maximum(i - CA_LEAD, 0) * CA_QBLOCK, CA_QBLOCK)
        q = (_rms(q_ref[pl.ds(r0, CA_QBLOCK), :].astype(F32), qg_ref[...]) * scale).astype(BF16)
        s = lax.dot_general(q, kn_ref[pl.ds(w0, CA_WINDOW), :], (((1,), (1,)), ((), ())),
                            preferred_element_type=F32)
        s = s + bias_ref[jnp.minimum(i, CA_LEAD), 0]
        p = jnp.exp(s - jnp.max(s, axis=-1, keepdims=True))
        denom = jnp.sum(p, axis=-1, keepdims=True)
        out = jnp.dot(p.astype(BF16), v_ref[pl.ds(w0, CA_WINDOW), :], preferred_element_type=F32)
        o_ref[pl.ds(r0, CA_QBLOCK), :] = (out / denom).astype(o_ref.dtype)
        return carry

    lax.fori_loop(0, n_blocks, body, 0, unroll=2)


def chunk_band_attention(proj, q_col, k_col, v_col, heads, bsz, seq, q_gain, k_gain, rel_bias):
    bias = _ca_bias(rel_bias)
    head_block = lambda col: pl.BlockSpec((seq, HEAD_DIM), lambda h, b: (b, col + h))
    gain = pl.BlockSpec((1, HEAD_DIM), lambda h, b: (0, 0))
    return pl.pallas_call(
        functools.partial(_ca_kernel, scale=HEAD_DIM ** -0.5),
        grid=(heads, bsz),
        in_specs=[head_block(q_col), head_block(k_col), head_block(v_col),
                  pl.BlockSpec((CA_LEAD + 1, 1, CA_QBLOCK, CA_WINDOW), lambda h, b: (0, h, 0, 0)),
                  gain, gain],
        out_specs=pl.BlockSpec((seq, HEAD_DIM), lambda h, b: (b, h)),
        out_shape=jax.ShapeDtypeStruct((bsz * seq, heads * HEAD_DIM), BF16),
        scratch_shapes=[pltpu.VMEM((seq, HEAD_DIM), BF16)],
        compiler_params=_params("parallel", "parallel"),
    )(proj, proj, proj, bias, q_gain.reshape(1, HEAD_DIM), k_gain.reshape(1, HEAD_DIM))


DA_BLOCK = 512


def _da_kernel(q1_ref, q2_ref, k1_ref, k2_ref, v_ref, qg_ref, kg_ref, lam_ref, slope_ref, sg_ref,
               o_ref, kn_ref, *, scale, lambda_init):
    tq = DA_BLOCK
    n_blocks = q1_ref.shape[0] // tq
    for c, k_ref in enumerate((k1_ref, k2_ref)):
        kn_ref[c] = _rms(k_ref[...].astype(F32), kg_ref[...]).astype(BF16)
    lam = (jnp.exp(jnp.sum(lam_ref[0:1, :] * lam_ref[1:2, :], axis=-1, keepdims=True))
           - jnp.exp(jnp.sum(lam_ref[2:3, :] * lam_ref[3:4, :], axis=-1, keepdims=True))
           + lambda_init)
    slope = slope_ref[0, :, 0:1]
    row = lax.broadcasted_iota(jnp.int32, (tq, tq), 0)
    col = lax.broadcasted_iota(jnp.int32, (tq, tq), 1)
    delta = (row - col).astype(F32)
    alibi_past = slope * delta
    alibi_diag = slope * jnp.abs(delta)
    allowed = (col // CHUNK) <= (row // CHUNK)

    def update(carry, s, v, shift):
        m, l, acc = carry
        m_new = jnp.maximum(m, jnp.max(s, axis=-1, keepdims=True) - shift)
        alpha = jnp.exp(m - m_new)
        p = jnp.exp(s - (m_new + shift))
        l = alpha * l + jnp.sum(p, axis=-1, keepdims=True)
        acc = alpha * acc + jnp.dot(p.astype(BF16), v, preferred_element_type=F32)
        return m_new, l, acc

    def scores(q, c, c0):
        return lax.dot_general(q, kn_ref[c, pl.ds(c0, tq), :], (((1,), (1,)), ((), ())),
                               preferred_element_type=F32)

    def q_block(i, carry):
        r0 = pl.multiple_of(i * tq, tq)
        qs = [(_rms(q_ref[pl.ds(r0, tq), :].astype(F32), qg_ref[...]) * scale).astype(BF16)
              for q_ref in (q1_ref, q2_ref)]

        def k_block(j, states):
            c0 = pl.multiple_of(j * tq, tq)
            gap = slope * (jnp.zeros((1, 1), jnp.int32) + (i - j) * tq).astype(F32)
            v = v_ref[pl.ds(c0, tq), :]
            return tuple(update(states[c], scores(qs[c], c, c0) - alibi_past, v, gap)
                         for c in range(2))

        init = (jnp.full((tq, 1), MASK_VALUE, F32), jnp.zeros((tq, 1), F32),
                jnp.zeros((tq, v_ref.shape[1]), F32))
        states = lax.fori_loop(0, i, k_block, (init, init))
        v = v_ref[pl.ds(r0, tq), :]
        outs = []
        for c in range(2):
            s = jnp.where(allowed, scores(qs[c], c, r0) - alibi_diag, MASK_VALUE)
            _, l, acc = update(states[c], s, v, 0.0)
            outs.append(acc / l)
        out = outs[0] - lam * outs[1]
        out = _rms(out, sg_ref[...]) * (1.0 - lambda_init)
        o_ref[pl.ds(r0, tq), :] = out.astype(o_ref.dtype)
        return carry

    lax.fori_loop(0, n_blocks, q_block, 0)


def diff_attention(proj, q_col, k_col, v_col, heads, bsz, seq, q_gain, k_gain, lam_vecs, subln_gain,
                   lambda_init):
    slopes = 2.0 ** (-8.0 * jnp.arange(1, heads + 1, dtype=F32) / heads)
    slopes = jnp.broadcast_to(slopes[:, None, None], (heads, 1, HEAD_DIM))
    qk_block = lambda col: pl.BlockSpec((seq, HEAD_DIM), lambda b, h: (b, col + 2 * h))
    vec = lambda n: pl.BlockSpec((1, n), lambda b, h: (0, 0))
    return pl.pallas_call(
        functools.partial(_da_kernel, scale=HEAD_DIM ** -0.5, lambda_init=lambda_init),
        grid=(bsz, heads),
        in_specs=[qk_block(q_col), qk_block(q_col + 1), qk_block(k_col), qk_block(k_col + 1),
                  pl.BlockSpec((seq, 2 * HEAD_DIM), lambda b, h: (b, v_col + h)),
                  vec(HEAD_DIM), vec(HEAD_DIM),
                  pl.BlockSpec((4, HEAD_DIM), lambda b, h: (0, 0)),
                  pl.BlockSpec((1, 1, HEAD_DIM), lambda b, h: (h, 0, 0)),
                  vec(2 * HEAD_DIM)],
        out_specs=pl.BlockSpec((seq, 2 * HEAD_DIM), lambda b, h: (b, h)),
        out_shape=jax.ShapeDtypeStruct((bsz * seq, heads * 2 * HEAD_DIM), BF16),
        scratch_shapes=[pltpu.VMEM((2, seq, HEAD_DIM), BF16)],
        compiler_params=_params("parallel", "parallel"),
    )(proj, proj, proj, proj, proj, q_gain.reshape(1, HEAD_DIM), k_gain.reshape(1, HEAD_DIM),
      lam_vecs, slopes, subln_gain.reshape(1, 2 * HEAD_DIM))


def _merge_kernel(ya_ref, yb_ref, yc_ref, wa_ref, wb_ref, wc_ref, ga_ref, gb_ref, gc_ref, o_ref):
    def branch(y_ref, w_ref, g_ref):
        out = jnp.dot(y_ref[...], w_ref[...], preferred_element_type=F32)
        return _sigmoid(g_ref[...].astype(F32)) * out

    merged = (branch(ya_ref, wa_ref, ga_ref) + branch(yb_ref, wb_ref, gb_ref)
              + branch(yc_ref, wc_ref, gc_ref))
    o_ref[...] = merged.astype(o_ref.dtype)


def gated_merge(ys, ws, proj, gate_col, *, tm=1024, tn=1024):
    t = ys[0].shape[0]
    d = ws[0].shape[1]
    tm, tn = _tile(t, tm, align=8), _tile(math.gcd(d, gate_col), tn)
    y_spec = lambda y: pl.BlockSpec((tm, y.shape[1]), lambda j, i: (i, 0))
    w_spec = lambda w: pl.BlockSpec((w.shape[0], tn), lambda j, i: (0, j))
    g_spec = lambda b: pl.BlockSpec((tm, tn), lambda j, i: (i, (gate_col + b * d) // tn + j))
    return pl.pallas_call(
        _merge_kernel,
        grid=(d // tn, t // tm),
        in_specs=[y_spec(y) for y in ys] + [w_spec(w) for w in ws] + [g_spec(b) for b in range(3)],
        out_specs=pl.BlockSpec((tm, tn), lambda j, i: (i, j)),
        out_shape=jax.ShapeDtypeStruct((t, d), BF16),
        compiler_params=_params("parallel", "parallel"),
    )(*ys, *ws, proj, proj, proj)


def kernel(x, norm_mix, w_in, ssm_a_re, ssm_a_im, ssm_log_dt, ssm_b_re, ssm_b_im, ssm_c_re, ssm_c_im, ssm_d, ssm_w_glu, ssm_b_glu, ca_q_gain, ca_k_gain, ca_rel_bias, da_q_gain, da_k_gain, da_lam_q1, da_lam_k1, da_lam_q2, da_lam_k2, da_subln_gain, w_out_a, w_out_b, w_out_c, w_o, norm_mlp, w_ff1, w_ff2):
    bsz, seq, d_model = x.shape
    depth = w_in.shape[0]
    ssm_w, ca_w, da_w = w_out_a.shape[1], w_out_b.shape[1], w_out_c.shape[1]
    groups = ssm_w // SSM_GROUP
    ca_heads, da_heads = ca_w // HEAD_DIM, da_w // (2 * HEAD_DIM)
    n_chunks = seq // SSM_CHUNK
    ca_q = ssm_w // HEAD_DIM
    ca_k, ca_v = ca_q + ca_heads, ca_q + 2 * ca_heads
    da_q = ca_q + 3 * ca_heads
    da_k, da_v = da_q + 2 * da_heads, da_q + 4 * da_heads
    gate_col = ssm_w + 3 * ca_w + 3 * da_w

    xt = x.reshape(bsz * seq, d_model)
    for l in range(depth):
        lambda_init = 0.8 - 0.6 * math.exp(-0.3 * l)
        h = rmsnorm(xt, norm_mix[l])
        proj = matmul(h, w_in[l].astype(BF16), out_dtype=BF16)

        u_g = proj[:, :ssm_w].reshape(bsz, n_chunks, SSM_CHUNK, groups, SSM_GROUP)
        u_g = u_g.transpose(3, 0, 1, 2, 4).reshape(groups, bsz * n_chunks, SSM_CHUNK * SSM_GROUP)
        mats = _s5_matrices(ssm_a_re[l], ssm_a_im[l], ssm_log_dt[l], ssm_b_re[l], ssm_b_im[l],
                            ssm_c_re[l], ssm_c_im[l], n_chunks)
        y_g = s5_scan(u_g, mats, n_chunks)
        y_s5 = y_g.reshape(groups, bsz, n_chunks, SSM_CHUNK, SSM_GROUP)
        y_s5 = y_s5.transpose(1, 2, 3, 0, 4).reshape(bsz * seq, ssm_w)
        y_a = s5_glu(y_s5, proj, ssm_d[l], ssm_w_glu[l].astype(BF16), ssm_b_glu[l])

        y_b = chunk_band_attention(proj, ca_q, ca_k, ca_v, ca_heads, bsz, seq,
                                   ca_q_gain[l], ca_k_gain[l], ca_rel_bias[l])
        lam_vecs = jnp.stack([da_lam_q1[l], da_lam_k1[l], da_lam_q2[l], da_lam_k2[l]]).astype(F32)
        y_c = diff_attention(proj, da_q, da_k, da_v // 2, da_heads, bsz, seq,
                             da_q_gain[l], da_k_gain[l], lam_vecs, da_subln_gain[l], lambda_init)

        merged = gated_merge((y_a, y_b, y_c),
                             (w_out_a[l].astype(BF16), w_out_b[l].astype(BF16),
                              w_out_c[l].astype(BF16)), proj, gate_col)
        xt = matmul(merged, w_o[l].astype(BF16), out_dtype=F32, epilogue="residual", residual=xt)
        h = rmsnorm(xt, norm_mlp[l])
        ff = matmul(h, w_ff1[l].astype(BF16), out_dtype=BF16, epilogue="relu2")
        xt = matmul(ff, w_ff2[l].astype(BF16), out_dtype=F32, epilogue="residual", residual=xt,
                    tk=2048)
    return xt.reshape(bsz, seq, d_model)
```

```python
import functools
import math

import jax
import jax.numpy as jnp
from jax import lax
from jax.experimental import pallas as pl
from jax.experimental.pallas import tpu as pltpu

F32 = jnp.float32
BF16 = jnp.bfloat16

EPS = 1e-6
HEAD_DIM = 128
CHUNK = 64
SSM_GROUP = 16
SSM_CHUNK = 16
S5_LANES = 128
CA_LEFT_CHUNKS = 8
REL_CLIP = 128
MASK_VALUE = -1e30

V7X_VMEM_BYTES = 64 * 1024 * 1024
VMEM_LIMIT = V7X_VMEM_BYTES - 4 * 1024 * 1024


def _params(*semantics):
    return pltpu.CompilerParams(dimension_semantics=semantics, vmem_limit_bytes=VMEM_LIMIT)


def _rms(x, gain):
    return x * lax.rsqrt(jnp.mean(x * x, axis=-1, keepdims=True) + EPS) * gain


def _sigmoid(x):
    return 1.0 / (1.0 + jnp.exp(-x))


def _tile(dim, preferred, align=128):
    best = None
    for t in range(align, min(dim, preferred) + 1, align):
        if dim % t == 0:
            best = t
    assert best is not None, (dim, preferred)
    return best


def _rmsnorm_kernel(x_ref, g_ref, o_ref):
    o_ref[...] = _rms(x_ref[...], g_ref[...]).astype(o_ref.dtype)


def rmsnorm(x, gain, *, tm=256):
    t, d = x.shape
    tm = _tile(t, tm, align=8)
    return pl.pallas_call(
        _rmsnorm_kernel,
        grid=(t // tm,),
        in_specs=[pl.BlockSpec((tm, d), lambda i: (i, 0)),
                  pl.BlockSpec((1, d), lambda i: (0, 0))],
        out_specs=pl.BlockSpec((tm, d), lambda i: (i, 0)),
        out_shape=jax.ShapeDtypeStruct((t, d), BF16),
        compiler_params=_params("parallel"),
    )(x, gain.reshape(1, d))


def _matmul_kernel(*refs, epilogue, nk):
    if epilogue == "residual":
        a_ref, w_ref, r_ref, o_ref = refs
    else:
        a_ref, w_ref, o_ref = refs
    acc = jnp.dot(a_ref[...], w_ref[...], preferred_element_type=F32)
    if nk == 1:
        if epilogue == "residual":
            acc = acc + r_ref[...]
        elif epilogue == "relu2":
            acc = jnp.square(jnp.maximum(acc, 0.0))
        o_ref[...] = acc.astype(o_ref.dtype)
    else:
        k = pl.program_id(2)

        @pl.when(k == 0)
        def _():
            o_ref[...] = acc + r_ref[...]

        @pl.when(k > 0)
        def _():
            o_ref[...] += acc


def matmul(a, w, *, out_dtype, epilogue="none", residual=None, tm=1024, tn=1024, tk=None):
    m, kdim = a.shape
    _, n = w.shape
    tm, tn = _tile(m, tm), _tile(n, tn)
    tk = kdim if tk is None else _tile(kdim, tk)
    nk = kdim // tk
    if nk > 1:
        assert epilogue == "residual" and out_dtype == F32
    in_specs = [pl.BlockSpec((tm, tk), lambda j, i, k: (i, k)),
                pl.BlockSpec((tk, tn), lambda j, i, k: (k, j))]
    args = [a, w]
    if epilogue == "residual":
        in_specs.append(pl.BlockSpec((tm, tn), lambda j, i, k: (i, j)))
        args.append(residual)
    return pl.pallas_call(
        functools.partial(_matmul_kernel, epilogue=epilogue, nk=nk),
        grid=(n // tn, m // tm, nk),
        in_specs=in_specs,
        out_specs=pl.BlockSpec((tm, tn), lambda j, i, k: (i, j)),
        out_shape=jax.ShapeDtypeStruct((m, n), out_dtype),
        compiler_params=_params("parallel", "parallel", "arbitrary"),
    )(*args)


def _s5_matrices(a_re, a_im, log_dt, b_re, b_im, c_re, c_im, n_chunks):
    L = SSM_CHUNK
    g, p = a_re.shape
    h = b_re.shape[-1]
    hi = lax.Precision.HIGHEST

    def cmul(xr, xi, yr, yi):
        return xr * yr - xi * yi, xr * yi + xi * yr

    lam_r, lam_i = jnp.minimum(a_re.astype(F32), -1e-4), a_im.astype(F32)
    dt = jnp.exp(log_dt.astype(F32))[:, None]
    z_r, z_i = lam_r * dt, lam_i * dt

    def a_pow(n):
        mag, ang = jnp.exp(z_r[..., None] * n), z_i[..., None] * n
        return mag * jnp.cos(ang), mag * jnp.sin(ang)

    ab_r, ab_i = jnp.exp(z_r) * jnp.cos(z_i), jnp.exp(z_r) * jnp.sin(z_i)
    den = lam_r * lam_r + lam_i * lam_i
    f_r = ((ab_r - 1.0) * lam_r + ab_i * lam_i) / den
    f_i = (ab_i * lam_r - (ab_r - 1.0) * lam_i) / den
    bb_r, bb_i = cmul(f_r[..., None], f_i[..., None], b_re.astype(F32), b_im.astype(F32))
    pw_r, pw_i = a_pow(jnp.arange(L + 1, dtype=F32))
    cr, ci = c_re.astype(F32)[..., None], c_im.astype(F32)[..., None]

    ca_r, ca_i = cmul(cr, ci, pw_r[:, None, :, :L], pw_i[:, None, :, :L])
    kern = (jnp.einsum("gipt,gpj->gtij", ca_r, bb_r, precision=hi)
            - jnp.einsum("gipt,gpj->gtij", ca_i, bb_i, precision=hi))
    ws_r, ws_i = cmul(pw_r[..., L - 1::-1][..., None], pw_i[..., L - 1::-1][..., None],
                      bb_r[:, :, None, :], bb_i[:, :, None, :])
    wc_r, wc_i = cmul(cr, ci, pw_r[:, None, :, 1:], pw_i[:, None, :, 1:])
    n_steps = int(math.log2(n_chunks))
    ak_r, ak_i = a_pow(L * 2.0 ** jnp.arange(n_steps, dtype=F32))

    gt = S5_LANES // h
    lt = g // gt
    eye = jnp.eye(gt, dtype=F32)
    k6 = kern.reshape(lt, gt, L, h, h).transpose(0, 2, 1, 4, 3)
    dcat = k6[:, :, :, :, None, :] * eye[None, None, :, None, :, None]
    dcat = dcat.reshape(lt, L * gt * h, gt * h)
    ws = jnp.stack([ws_r, ws_i]).reshape(2, lt, gt, p, L, h).transpose(1, 4, 2, 5, 0, 3)
    wsum = ws[:, :, :, :, :, None, :] * eye[None, None, :, None, None, :, None]
    wsum = wsum.reshape(lt, L * gt * h, 2 * gt * p)
    wc = jnp.stack([wc_r, -wc_i]).reshape(2, lt, gt, h, p, L).transpose(1, 0, 2, 4, 5, 3)
    wcar = wc[:, :, :, :, :, None, :] * eye[None, None, :, None, None, :, None]
    wcar = wcar.reshape(lt, 2 * gt * p, L * gt * h)
    tile_cols = lambda a: a.reshape(lt, gt, p, n_steps).transpose(0, 3, 1, 2).reshape(lt, n_steps, gt * p)
    ak_r, ak_i = tile_cols(ak_r), tile_cols(ak_i)
    ar = jnp.concatenate([ak_r, ak_r], axis=-1)
    ai = jnp.concatenate([-ak_i, ak_i], axis=-1)
    return dcat.astype(BF16), wsum.astype(BF16), wcar.astype(BF16), ar, ai


def _s5_kernel(u_ref, dcat_ref, wsum_ref, wcar_ref, ar_ref, ai_ref, y_ref, x_ref, *, row_tile):
    L = SSM_CHUNK
    seq, lanes = u_ref.shape
    n_chunks = seq // L
    step = lax.broadcasted_iota(jnp.int32, (row_tile, lanes), 0) % L

    def intra(i, carry):
        r0 = pl.multiple_of(i * row_tile, row_tile)
        x = u_ref[pl.ds(r0, row_tile), :].astype(F32)
        x_ref[pl.ds(r0, row_tile), :] = x
        lagged = [x.astype(BF16)] + [
            jnp.where(step >= lag, pltpu.roll(x, lag, 0), 0.0).astype(BF16) for lag in range(1, L)]
        y_ref[pl.ds(r0, row_tile), :] = jnp.dot(jnp.concatenate(lagged, axis=1), dcat_ref[0],
                                                preferred_element_type=F32)
        return carry

    lax.fori_loop(0, seq // row_tile, intra, 0)

    steps = [x_ref[pl.ds(s, n_chunks, stride=L), :].astype(BF16) for s in range(L)]
    state = jnp.dot(jnp.concatenate(steps, axis=1), wsum_ref[0], preferred_element_type=F32)
    half = state.shape[-1] // 2
    chunk = lax.broadcasted_iota(jnp.int32, state.shape, 0)
    for k in range(ar_ref.shape[1]):
        shift = 1 << k
        prev = jnp.where(chunk >= shift, pltpu.roll(state, shift, 0), 0.0)
        state = (state + ar_ref[0, k:k + 1, :] * prev
                 + ai_ref[0, k:k + 1, :] * pltpu.roll(prev, half, 1))
    incoming = jnp.where(chunk >= 1, pltpu.roll(state, 1, 0), 0.0)
    carried = jnp.dot(incoming.astype(BF16), wcar_ref[0], preferred_element_type=F32)
    for t in range(L):
        rows = pl.ds(t, n_chunks, stride=L)
        y_ref[rows, :] = y_ref[rows, :] + carried[:, t * lanes:(t + 1) * lanes]


def s5_scan(proj, mats, bsz, seq, width, *, row_tile=512):
    dcat, wsum, wcar, ar, ai = mats
    blk = lambda a: pl.BlockSpec((1,) + a.shape[1:], lambda j, b: (j, 0, 0))
    return pl.pallas_call(
        functools.partial(_s5_kernel, row_tile=_tile(seq, row_tile)),
        grid=(width // S5_LANES, bsz),
        in_specs=[pl.BlockSpec((seq, S5_LANES), lambda j, b: (b, j)),
                  blk(dcat), blk(wsum), blk(wcar), blk(ar), blk(ai)],
        out_specs=pl.BlockSpec((seq, S5_LANES), lambda j, b: (b, j)),
        out_shape=jax.ShapeDtypeStruct((bsz * seq, width), F32),
        scratch_shapes=[pltpu.VMEM((seq, S5_LANES), F32)],
        compiler_params=_params("parallel", "parallel"),
    )(proj, dcat, wsum, wcar, ar, ai)


def _s5_glu_kernel(y_ref, u_ref, d_ref, w_ref, b_ref, o_ref):
    y = y_ref[...] + d_ref[...] * u_ref[...].astype(F32)
    z = y * (0.5 * (1.0 + jnp.tanh(math.sqrt(2.0 / math.pi) * (y + 0.044715 * (y * y * y)))))
    gate = jnp.dot(z.astype(BF16), w_ref[...], preferred_element_type=F32) + b_ref[...]
    o_ref[...] = (z * _sigmoid(gate)).astype(o_ref.dtype)


def s5_glu(y, proj, d_skip, w_glu, b_glu, *, tm=512):
    t, w = y.shape
    tm = _tile(t, tm, align=8)
    row = lambda a: pl.BlockSpec((1, w), lambda i: (0, 0))
    return pl.pallas_call(
        _s5_glu_kernel,
        grid=(t // tm,),
        in_specs=[pl.BlockSpec((tm, w), lambda i: (i, 0)),
                  pl.BlockSpec((tm, w), lambda i: (i, 0)),
                  row(d_skip),
                  pl.BlockSpec((w, w), lambda i: (0, 0)),
                  row(b_glu)],
        out_specs=pl.BlockSpec((tm, w), lambda i: (i, 0)),
        out_shape=jax.ShapeDtypeStruct((t, w), BF16),
        compiler_params=_params("parallel"),
    )(y, proj, d_skip.reshape(1, w), w_glu, b_glu.reshape(1, w))


CA_QBLOCK = 2 * CHUNK
CA_WINDOW = CA_LEFT_CHUNKS * CHUNK + CA_QBLOCK
CA_LEAD = CA_LEFT_CHUNKS * CHUNK // CA_QBLOCK


def _ca_bias(rel_bias):
    heads = rel_bias.shape[0]
    o = jnp.arange(CA_LEAD + 1)[:, None, None]
    q_pos = o * CA_QBLOCK + jnp.arange(CA_QBLOCK)[None, :, None]
    k_pos = jnp.arange(CA_WINDOW)[None, None, :]
    q_chunk, k_chunk = q_pos // CHUNK, k_pos // CHUNK
    allowed = (k_chunk <= q_chunk) & (k_chunk >= q_chunk - CA_LEFT_CHUNKS)
    period = CA_WINDOW + CA_QBLOCK
    n = jnp.arange(period)
    k_minus_q = jnp.where(n < CA_WINDOW, n, n - period)
    idx = jnp.clip(o[:, :, 0] * CA_QBLOCK - k_minus_q[None, :], -REL_CLIP, REL_CLIP) + REL_CLIP
    line = rel_bias.astype(F32)[:, idx]
    tiled = jnp.tile(line, (1, 1, CA_QBLOCK))[..., :CA_QBLOCK * (period - 1)]
    bias = tiled.reshape(heads, CA_LEAD + 1, CA_QBLOCK, period - 1)[..., :CA_WINDOW]
    return jnp.where(allowed[None], bias, MASK_VALUE).transpose(1, 0, 2, 3)


def _ca_kernel(q_ref, k_ref, v_ref, bias_ref, qg_ref, kg_ref, o_ref, kn_ref, *, scale):
    kn_ref[...] = _rms(k_ref[...].astype(F32), kg_ref[...]).astype(BF16)
    n_blocks = q_ref.shape[0] // CA_QBLOCK

    def body(i, carry):
        r0 = pl.multiple_of(i * CA_QBLOCK, CA_QBLOCK)
        w0 = pl.multiple_of(jnp.maximum(i - CA_LEAD, 0) * CA_QBLOCK, CA_QBLOCK)
        q = (_rms(q_ref[pl.ds(r0, CA_QBLOCK), :].astype(F32), qg_ref[...]) * scale).astype(BF16)
        s = lax.dot_general(q, kn_ref[pl.ds(w0, CA_WINDOW), :], (((1,), (1,)), ((), ())),
                            preferred_element_type=F32)
        s = s + bias_ref[jnp.minimum(i, CA_LEAD), 0]
        p = jnp.exp(s - jnp.max(s, axis=-1, keepdims=True))
        denom = jnp.sum(p, axis=-1, keepdims=True)
        out = jnp.dot(p.astype(BF16), v_ref[pl.ds(w0, CA_WINDOW), :], preferred_element_type=F32)
        o_ref[pl.ds(r0, CA_QBLOCK), :] = (out / denom).astype(o_ref.dtype)
        return carry

    lax.fori_loop(0, n_blocks, body, 0, unroll=4)


def chunk_band_attention(proj, q_col, k_col, v_col, heads, bsz, seq, q_gain, k_gain, rel_bias):
    bias = _ca_bias(rel_bias)
    head_block = lambda col: pl.BlockSpec((seq, HEAD_DIM), lambda h, b: (b, col + h))
    gain = pl.BlockSpec((1, HEAD_DIM), lambda h, b: (0, 0))
    return pl.pallas_call(
        functools.partial(_ca_kernel, scale=HEAD_DIM ** -0.5),
        grid=(heads, bsz),
        in_specs=[head_block(q_col), head_block(k_col), head_block(v_col),
                  pl.BlockSpec((CA_LEAD + 1, 1, CA_QBLOCK, CA_WINDOW), lambda h, b: (0, h, 0, 0)),
                  gain, gain],
        out_specs=pl.BlockSpec((seq, HEAD_DIM), lambda h, b: (b, h)),
        out_shape=jax.ShapeDtypeStruct((bsz * seq, heads * HEAD_DIM), BF16),
        scratch_shapes=[pltpu.VMEM((seq, HEAD_DIM), BF16)],
        compiler_params=_params("parallel", "parallel"),
    )(proj, proj, proj, bias, q_gain.reshape(1, HEAD_DIM), k_gain.reshape(1, HEAD_DIM))


DA_BLOCK = 512


def _da_kernel(q1_ref, q2_ref, k1_ref, k2_ref, v_ref, qg_ref, kg_ref, lam_ref, slope_ref, sg_ref,
               o_ref, kn_ref, *, scale, lambda_init):
    tq = DA_BLOCK
    n_blocks = q1_ref.shape[0] // tq
    for c, k_ref in enumerate((k1_ref, k2_ref)):
        kn_ref[c] = _rms(k_ref[...].astype(F32), kg_ref[...]).astype(BF16)
    lam = (jnp.exp(jnp.sum(lam_ref[0:1, :] * lam_ref[1:2, :], axis=-1, keepdims=True))
           - jnp.exp(jnp.sum(lam_ref[2:3, :] * lam_ref[3:4, :], axis=-1, keepdims=True))
           + lambda_init)
    slope = slope_ref[0, :, 0:1]
    row = lax.broadcasted_iota(jnp.int32, (tq, tq), 0)
    col = lax.broadcasted_iota(jnp.int32, (tq, tq), 1)
    delta = (row - col).astype(F32)
    alibi_past = slope * delta
    alibi_diag = slope * jnp.abs(delta)
    allowed = (col // CHUNK) <= (row // CHUNK)

    def update(carry, s, v, shift):
        m, l, acc = carry
        m_new = jnp.maximum(m, jnp.max(s, axis=-1, keepdims=True) - shift)
        alpha = jnp.exp(m - m_new)
        p = jnp.exp(s - (m_new + shift))
        l = alpha * l + jnp.sum(p, axis=-1, keepdims=True)
        acc = alpha * acc + jnp.dot(p.astype(BF16), v, preferred_element_type=F32)
        return m_new, l, acc

    def scores(q, c, c0):
        return lax.dot_general(q, kn_ref[c, pl.ds(c0, tq), :], (((1,), (1,)), ((), ())),
                               preferred_element_type=F32)

    def q_block(i, carry):
        r0 = pl.multiple_of(i * tq, tq)
        qs = [(_rms(q_ref[pl.ds(r0, tq), :].astype(F32), qg_ref[...]) * scale).astype(BF16)
              for q_ref in (q1_ref, q2_ref)]

        def k_block(j, states):
            c0 = pl.multiple_of(j * tq, tq)
            gap = slope * (jnp.zeros((1, 1), jnp.int32) + (i - j) * tq).astype(F32)
            v = v_ref[pl.ds(c0, tq), :]
            return tuple(update(states[c], scores(qs[c], c, c0) - alibi_past, v, gap)
                         for c in range(2))

        init = (jnp.full((tq, 1), MASK_VALUE, F32), jnp.zeros((tq, 1), F32),
                jnp.zeros((tq, v_ref.shape[1]), F32))
        states = lax.fori_loop(0, i, k_block, (init, init))
        v = v_ref[pl.ds(r0, tq), :]
        outs = []
        for c in range(2):
            s = jnp.where(allowed, scores(qs[c], c, r0) - alibi_diag, MASK_VALUE)
            _, l, acc = update(states[c], s, v, 0.0)
            outs.append(acc / l)
        out = outs[0] - lam * outs[1]
        out = _rms(out, sg_ref[...]) * (1.0 - lambda_init)
        o_ref[pl.ds(r0, tq), :] = out.astype(o_ref.dtype)
        return carry

    lax.fori_loop(0, n_blocks, q_block, 0)


def diff_attention(proj, q_col, k_col, v_col, heads, bsz, seq, q_gain, k_gain, lam_vecs, subln_gain,
                   lambda_init):
    slopes = 2.0 ** (-8.0 * jnp.arange(1, heads + 1, dtype=F32) / heads)
    slopes = jnp.broadcast_to(slopes[:, None, None], (heads, 1, HEAD_DIM))
    qk_block = lambda col: pl.BlockSpec((seq, HEAD_DIM), lambda b, h: (b, col + 2 * h))
    vec = lambda n: pl.BlockSpec((1, n), lambda b, h: (0, 0))
    return pl.pallas_call(
        functools.partial(_da_kernel, scale=HEAD_DIM ** -0.5, lambda_init=lambda_init),
        grid=(bsz, heads),
        in_specs=[qk_block(q_col), qk_block(q_col + 1), qk_block(k_col), qk_block(k_col + 1),
                  pl.BlockSpec((seq, 2 * HEAD_DIM), lambda b, h: (b, v_col + h)),
                  vec(HEAD_DIM), vec(HEAD_DIM),
                  pl.BlockSpec((4, HEAD_DIM), lambda b, h: (0, 0)),
                  pl.BlockSpec((1, 1, HEAD_DIM), lambda b, h: (h, 0, 0)),
                  vec(2 * HEAD_DIM)],
        out_specs=pl.BlockSpec((seq, 2 * HEAD_DIM), lambda b, h: (b, h)),
        out_shape=jax.ShapeDtypeStruct((bsz * seq, heads * 2 * HEAD_DIM), BF16),
        scratch_shapes=[pltpu.VMEM((2, seq, HEAD_DIM), BF16)],
        compiler_params=_params("parallel", "parallel"),
    )(proj, proj, proj, proj, proj, q_gain.reshape(1, HEAD_DIM), k_gain.reshape(1, HEAD_DIM),
      lam_vecs, slopes, subln_gain.reshape(1, 2 * HEAD_DIM))


def _merge_kernel(ya_ref, yb_ref, yc_ref, wa_ref, wb_ref, wc_ref, ga_ref, gb_ref, gc_ref, o_ref):
    def branch(y_ref, w_ref, g_ref):
        out = jnp.dot(y_ref[...], w_ref[...], preferred_element_type=F32)
        return _sigmoid(g_ref[...].astype(F32)) * out

    merged = (branch(ya_ref, wa_ref, ga_ref) + branch(yb_ref, wb_ref, gb_ref)
              + branch(yc_ref, wc_ref, gc_ref))
    o_ref[...] = merged.astype(o_ref.dtype)


def gated_merge(ys, ws, proj, gate_col, *, tm=1024, tn=1024):
    t = ys[0].shape[0]
    d = ws[0].shape[1]
    tm, tn = _tile(t, tm, align=8), _tile(math.gcd(d, gate_col), tn)
    y_spec = lambda y: pl.BlockSpec((tm, y.shape[1]), lambda j, i: (i, 0))
    w_spec = lambda w: pl.BlockSpec((w.shape[0], tn), lambda j, i: (0, j))
    g_spec = lambda b: pl.BlockSpec((tm, tn), lambda j, i: (i, (gate_col + b * d) // tn + j))
    return pl.pallas_call(
        _merge_kernel,
        grid=(d // tn, t // tm),
        in_specs=[y_spec(y) for y in ys] + [w_spec(w) for w in ws] + [g_spec(b) for b in range(3)],
        out_specs=pl.BlockSpec((tm, tn), lambda j, i: (i, j)),
        out_shape=jax.ShapeDtypeStruct((t, d), BF16),
        compiler_params=_params("parallel", "parallel"),
    )(*ys, *ws, proj, proj, proj)


def kernel(x, norm_mix, w_in, ssm_a_re, ssm_a_im, ssm_log_dt, ssm_b_re, ssm_b_im, ssm_c_re, ssm_c_im, ssm_d, ssm_w_glu, ssm_b_glu, ca_q_gain, ca_k_gain, ca_rel_bias, da_q_gain, da_k_gain, da_lam_q1, da_lam_k1, da_lam_q2, da_lam_k2, da_subln_gain, w_out_a, w_out_b, w_out_c, w_o, norm_mlp, w_ff1, w_ff2):
    bsz, seq, d_model = x.shape
    depth = w_in.shape[0]
    ssm_w, ca_w, da_w = w_out_a.shape[1], w_out_b.shape[1], w_out_c.shape[1]
    ca_heads, da_heads = ca_w // HEAD_DIM, da_w // (2 * HEAD_DIM)
    n_chunks = seq // SSM_CHUNK
    ca_q = ssm_w // HEAD_DIM
    ca_k, ca_v = ca_q + ca_heads, ca_q + 2 * ca_heads
    da_q = ca_q + 3 * ca_heads
    da_k, da_v = da_q + 2 * da_heads, da_q + 4 * da_heads
    gate_col = ssm_w + 3 * ca_w + 3 * da_w

    xt = x.reshape(bsz * seq, d_model)
    for l in range(depth):
        lambda_init = 0.8 - 0.6 * math.exp(-0.3 * l)
        h = rmsnorm(xt, norm_mix[l])
        proj = matmul(h, w_in[l].astype(BF16), out_dtype=BF16)

        mats = _s5_matrices(ssm_a_re[l], ssm_a_im[l], ssm_log_dt[l], ssm_b_re[l], ssm_b_im[l],
                            ssm_c_re[l], ssm_c_im[l], n_chunks)
        y_s5 = s5_scan(proj, mats, bsz, seq, ssm_w)
        y_a = s5_glu(y_s5, proj, ssm_d[l], ssm_w_glu[l].astype(BF16), ssm_b_glu[l])

        y_b = chunk_band_attention(proj, ca_q, ca_k, ca_v, ca_heads, bsz, seq,
                                   ca_q_gain[l], ca_k_gain[l], ca_rel_bias[l])
        lam_vecs = jnp.stack([da_lam_q1[l], da_lam_k1[l], da_lam_q2[l], da_lam_k2[l]]).astype(F32)
        y_c = diff_attention(proj, da_q, da_k, da_v // 2, da_heads, bsz, seq,
                             da_q_gain[l], da_k_gain[l], lam_vecs, da_subln_gain[l], lambda_init)

        merged = gated_merge((y_a, y_b, y_c),
                             (w_out_a[l].astype(BF16), w_out_b[l].astype(BF16),
                              w_out_c[l].astype(BF16)), proj, gate_col)
        xt = matmul(merged, w_o[l].astype(BF16), out_dtype=F32, epilogue="residual", residual=xt)
        h = rmsnorm(xt, norm_mlp[l])
        ff = matmul(h, w_ff1[l].astype(BF16), out_dtype=BF16, epilogue="relu2")
        xt = matmul(ff, w_ff2[l].astype(BF16), out_dtype=F32, epilogue="residual", residual=xt,
                    tk=4096)
    return xt.reshape(bsz, seq, d_model)
```

```python
import functools
import math

import jax
import jax.numpy as jnp
from jax import lax
from jax.experimental import pallas as pl
from jax.experimental.pallas import tpu as pltpu

F32 = jnp.float32
BF16 = jnp.bfloat16

EPS = 1e-6
HEAD_DIM = 128
CHUNK = 64
SSM_GROUP = 16
SSM_CHUNK = 16
S5_LANES = 128
CA_LEFT_CHUNKS = 8
REL_CLIP = 128
MASK_VALUE = -1e30
LOG2E = math.log2(math.e)

V7X_VMEM_BYTES = 64 * 1024 * 1024
VMEM_LIMIT = V7X_VMEM_BYTES - 4 * 1024 * 1024


def _params(*semantics):
    return pltpu.CompilerParams(dimension_semantics=semantics, vmem_limit_bytes=VMEM_LIMIT)


def _rms(x, gain):
    return x * lax.rsqrt(jnp.mean(x * x, axis=-1, keepdims=True) + EPS) * gain


def _sigmoid(x):
    return 1.0 / (1.0 + jnp.exp(-x))


def _tile(dim, preferred, align=128):
    best = None
    for t in range(align, min(dim, preferred) + 1, align):
        if dim % t == 0:
            best = t
    assert best is not None, (dim, preferred)
    return best


def _rmsnorm_kernel(x_ref, g_ref, o_ref):
    o_ref[...] = _rms(x_ref[...], g_ref[...]).astype(o_ref.dtype)


def rmsnorm(x, gain, *, tm=256):
    t, d = x.shape
    tm = _tile(t, tm, align=8)
    return pl.pallas_call(
        _rmsnorm_kernel,
        grid=(t // tm,),
        in_specs=[pl.BlockSpec((tm, d), lambda i: (i, 0)),
                  pl.BlockSpec((1, d), lambda i: (0, 0))],
        out_specs=pl.BlockSpec((tm, d), lambda i: (i, 0)),
        out_shape=jax.ShapeDtypeStruct((t, d), BF16),
        compiler_params=_params("parallel"),
    )(x, gain.reshape(1, d))


def _matmul_kernel(*refs, epilogue, nk):
    if epilogue == "residual":
        a_ref, w_ref, r_ref, o_ref = refs
    else:
        a_ref, w_ref, o_ref = refs
    acc = jnp.dot(a_ref[...], w_ref[...], preferred_element_type=F32)
    if nk == 1:
        if epilogue == "residual":
            acc = acc + r_ref[...]
        elif epilogue == "relu2":
            acc = jnp.square(jnp.maximum(acc, 0.0))
        o_ref[...] = acc.astype(o_ref.dtype)
    else:
        k = pl.program_id(2)

        @pl.when(k == 0)
        def _():
            o_ref[...] = acc + r_ref[...]

        @pl.when(k > 0)
        def _():
            o_ref[...] += acc


def matmul(a, w, layer, *, out_dtype, epilogue="none", residual=None, tm=1024, tn=1024, tk=None):
    m, kdim = a.shape
    _, _, n = w.shape
    tm, tn = _tile(m, tm), _tile(n, tn)
    tk = kdim if tk is None else _tile(kdim, tk)
    nk = kdim // tk
    if nk > 1:
        assert epilogue == "residual" and out_dtype == F32
    in_specs = [pl.BlockSpec((tm, tk), lambda j, i, k: (i, k)),
                pl.BlockSpec((None, tk, tn), lambda j, i, k: (layer, k, j))]
    args = [a, w]
    if epilogue == "residual":
        in_specs.append(pl.BlockSpec((tm, tn), lambda j, i, k: (i, j)))
        args.append(residual)
    return pl.pallas_call(
        functools.partial(_matmul_kernel, epilogue=epilogue, nk=nk),
        grid=(n // tn, m // tm, nk),
        in_specs=in_specs,
        out_specs=pl.BlockSpec((tm, tn), lambda j, i, k: (i, j)),
        out_shape=jax.ShapeDtypeStruct((m, n), out_dtype),
        compiler_params=_params("parallel", "parallel", "arbitrary"),
    )(*args)


def _s5_matrices(a_re, a_im, log_dt, b_re, b_im, c_re, c_im, n_chunks):
    L = SSM_CHUNK
    g, p = a_re.shape
    h = b_re.shape[-1]
    gt = S5_LANES // h
    lt = g // gt

    def cmul(xr, xi, yr, yi):
        return xr * yr - xi * yi, xr * yi + xi * yr

    lam_r, lam_i = jnp.minimum(a_re.astype(F32), -1e-4), a_im.astype(F32)
    dt = jnp.exp(log_dt.astype(F32))[:, None]
    z_r, z_i = lam_r * dt, lam_i * dt

    def a_pow(n):
        mag, ang = jnp.exp(z_r[:, None, :] * n[None, :, None]), z_i[:, None, :] * n[None, :, None]
        return mag * jnp.cos(ang), mag * jnp.sin(ang)

    ab_r, ab_i = jnp.exp(z_r) * jnp.cos(z_i), jnp.exp(z_r) * jnp.sin(z_i)
    den = lam_r * lam_r + lam_i * lam_i
    f_r = ((ab_r - 1.0) * lam_r + ab_i * lam_i) / den
    f_i = (ab_i * lam_r - (ab_r - 1.0) * lam_i) / den
    bb_r, bb_i = cmul(f_r[:, None, :], f_i[:, None, :],
                      b_re.astype(F32).transpose(0, 2, 1), b_im.astype(F32).transpose(0, 2, 1))
    pw_r, pw_i = a_pow(jnp.arange(L + 1, dtype=F32))
    cr, ci = c_re.astype(F32)[:, None], c_im.astype(F32)[:, None]

    ca_r, ca_i = cmul(cr, ci, pw_r[:, :L, None, :], pw_i[:, :L, None, :])
    kern = jnp.einsum("glip,gjp->glji", ca_r, bb_r, precision=lax.Precision.HIGHEST)
    kern = kern - jnp.einsum("glip,gjp->glji", ca_i, bb_i, precision=lax.Precision.HIGHEST)
    ws_r, ws_i = cmul(pw_r[:, L - 1::-1, None, :], pw_i[:, L - 1::-1, None, :],
                      bb_r[:, None], bb_i[:, None])
    wc_r, wc_i = cmul(cr, ci, pw_r[:, 1:, None, :], pw_i[:, 1:, None, :])

    def tile_rows(a):
        n = a.shape[-1]
        return a.reshape(lt, gt, L, h, n).transpose(0, 2, 1, 3, 4).reshape(lt, L * gt * h, n)

    kern = tile_rows(kern)
    wsum = tile_rows(jnp.concatenate([ws_r, ws_i], axis=-1))
    wcar = tile_rows(jnp.concatenate([wc_r, -wc_i], axis=-1))
    n_steps = int(math.log2(n_chunks))
    ak_r, ak_i = a_pow(L * 2.0 ** jnp.arange(n_steps, dtype=F32))
    tile_cols = lambda a: a.reshape(lt, gt, n_steps, 2 * p).transpose(0, 2, 1, 3).reshape(
        lt, n_steps, gt * 2 * p)
    ar = tile_cols(jnp.concatenate([ak_r, ak_r], axis=-1))
    ai = tile_cols(jnp.concatenate([-ak_i, ak_i], axis=-1))
    return kern.astype(BF16), wsum.astype(BF16), wcar.astype(BF16), ar, ai


def _s5_kernel(u_ref, kern_ref, wsum_ref, wcar_ref, ar_ref, ai_ref, y_ref,
               x_ref, dcat_ref, wsum_full_ref, wcar_full_ref, *, row_tile):
    L = SSM_CHUNK
    seq, lanes = u_ref.shape
    n_chunks = seq // L
    h = kern_ref.shape[-1]
    gt = lanes // h
    state_w = wsum_ref.shape[-1]

    @pl.when(pl.program_id(1) == 0)
    def _():
        rows = kern_ref.shape[1]
        row_group = (lax.broadcasted_iota(jnp.int32, (rows, lanes), 0) // h) % gt
        lane = lax.broadcasted_iota(jnp.int32, (rows, lanes), 1)
        spread = (lax.broadcasted_iota(jnp.int32, (h, lanes), 1) % h
                  == lax.broadcasted_iota(jnp.int32, (h, lanes), 0)).astype(BF16)
        tiled = jnp.dot(kern_ref[0], spread, preferred_element_type=F32)
        dcat_ref[...] = jnp.where(row_group == lane // h, tiled, 0.0).astype(BF16)
        for grp in range(gt):
            cols = slice(grp * state_w, (grp + 1) * state_w)
            wsum_full_ref[:, cols] = jnp.where(row_group == grp, wsum_ref[0], 0.0)
            wcar_full_ref[:, cols] = jnp.where(row_group == grp, wcar_ref[0], 0.0)

    step = lax.broadcasted_iota(jnp.int32, (row_tile, lanes), 0) % L

    def intra(i, carry):
        r0 = pl.multiple_of(i * row_tile, row_tile)
        x = u_ref[pl.ds(r0, row_tile), :].astype(F32)
        x_ref[pl.ds(r0, row_tile), :] = x
        lagged = [x.astype(BF16)] + [
            jnp.where(step >= lag, pltpu.roll(x, lag, 0), 0.0).astype(BF16) for lag in range(1, L)]
        y_ref[pl.ds(r0, row_tile), :] = jnp.dot(jnp.concatenate(lagged, axis=1), dcat_ref[...],
                                                preferred_element_type=F32)
        return carry

    lax.fori_loop(0, seq // row_tile, intra, 0)

    steps = [x_ref[pl.ds(s, n_chunks, stride=L), :].astype(BF16) for s in range(L)]
    state = jnp.dot(jnp.concatenate(steps, axis=1), wsum_full_ref[...],
                    preferred_element_type=F32)
    chunk = lax.broadcasted_iota(jnp.int32, state.shape, 0)

    def swap_re_im(a):
        return jnp.concatenate(
            [pltpu.roll(a[:, grp * state_w:(grp + 1) * state_w], state_w // 2, 1)
             for grp in range(gt)], axis=1)

    for k in range(ar_ref.shape[1]):
        shift = 1 << k
        prev = jnp.where(chunk >= shift, pltpu.roll(state, shift, 0), 0.0)
        state = state + ar_ref[0, k:k + 1, :] * prev + ai_ref[0, k:k + 1, :] * swap_re_im(prev)
    incoming = jnp.where(chunk >= 1, pltpu.roll(state, 1, 0), 0.0)
    carried = lax.dot_general(incoming.astype(BF16), wcar_full_ref[...], (((1,), (1,)), ((), ())),
                              preferred_element_type=F32)
    for t in range(L):
        rows = pl.ds(t, n_chunks, stride=L)
        y_ref[rows, :] = y_ref[rows, :] + carried[:, t * lanes:(t + 1) * lanes]


def s5_scan(proj, mats, bsz, seq, width, *, row_tile=512):
    kern, wsum, wcar, ar, ai = mats
    rows, state_w = wsum.shape[1:]
    gt = S5_LANES // kern.shape[-1]
    assert state_w == S5_LANES
    blk = lambda a: pl.BlockSpec((1,) + a.shape[1:], lambda j, b: (j, 0, 0))
    return pl.pallas_call(
        functools.partial(_s5_kernel, row_tile=_tile(seq, row_tile)),
        grid=(width // S5_LANES, bsz),
        in_specs=[pl.BlockSpec((seq, S5_LANES), lambda j, b: (b, j)),
                  blk(kern), blk(wsum), blk(wcar), blk(ar), blk(ai)],
        out_specs=pl.BlockSpec((seq, S5_LANES), lambda j, b: (b, j)),
        out_shape=jax.ShapeDtypeStruct((bsz * seq, width), F32),
        scratch_shapes=[pltpu.VMEM((seq, S5_LANES), F32),
                        pltpu.VMEM((rows, S5_LANES), BF16),
                        pltpu.VMEM((rows, gt * state_w), BF16),
                        pltpu.VMEM((rows, gt * state_w), BF16)],
        compiler_params=_params("arbitrary", "arbitrary"),
    )(proj, kern, wsum, wcar, ar, ai)


def _s5_glu_kernel(y_ref, u_ref, d_ref, w_ref, b_ref, o_ref):
    y = y_ref[...] + d_ref[...] * u_ref[...].astype(F32)
    z = y * (0.5 * (1.0 + jnp.tanh(math.sqrt(2.0 / math.pi) * (y + 0.044715 * (y * y * y)))))
    gate = jnp.dot(z.astype(BF16), w_ref[...], preferred_element_type=F32) + b_ref[...]
    o_ref[...] = (z * _sigmoid(gate)).astype(o_ref.dtype)


def s5_glu(y, proj, d_skip, w_glu, layer, b_glu, *, tm=512):
    t, w = y.shape
    tm = _tile(t, tm, align=8)
    row = lambda a: pl.BlockSpec((1, w), lambda i: (0, 0))
    return pl.pallas_call(
        _s5_glu_kernel,
        grid=(t // tm,),
        in_specs=[pl.BlockSpec((tm, w), lambda i: (i, 0)),
                  pl.BlockSpec((tm, w), lambda i: (i, 0)),
                  row(d_skip),
                  pl.BlockSpec((None, w, w), lambda i: (layer, 0, 0)),
                  row(b_glu)],
        out_specs=pl.BlockSpec((tm, w), lambda i: (i, 0)),
        out_shape=jax.ShapeDtypeStruct((t, w), BF16),
        compiler_params=_params("parallel"),
    )(y, proj, d_skip.reshape(1, w), w_glu, b_glu.reshape(1, w))


CA_QBLOCK = 2 * CHUNK
CA_WINDOW = CA_LEFT_CHUNKS * CHUNK + CA_QBLOCK
CA_LEAD = CA_LEFT_CHUNKS * CHUNK // CA_QBLOCK


def _ca_bias(rel_bias):
    heads = rel_bias.shape[0]
    o = jnp.arange(CA_LEAD + 1)[:, None, None]
    q_pos = o * CA_QBLOCK + jnp.arange(CA_QBLOCK)[None, :, None]
    k_pos = jnp.arange(CA_WINDOW)[None, None, :]
    q_chunk, k_chunk = q_pos // CHUNK, k_pos // CHUNK
    allowed = (k_chunk <= q_chunk) & (k_chunk >= q_chunk - CA_LEFT_CHUNKS)
    period = CA_WINDOW + CA_QBLOCK
    n = jnp.arange(period)
    k_minus_q = jnp.where(n < CA_WINDOW, n, n - period)
    idx = jnp.clip(o[:, :, 0] * CA_QBLOCK - k_minus_q[None, :], -REL_CLIP, REL_CLIP) + REL_CLIP
    line = LOG2E * rel_bias.astype(F32)[:, idx]
    tiled = jnp.tile(line, (1, 1, CA_QBLOCK))[..., :CA_QBLOCK * (period - 1)]
    bias = tiled.reshape(heads, CA_LEAD + 1, CA_QBLOCK, period - 1)[..., :CA_WINDOW]
    return jnp.where(allowed[None], bias, MASK_VALUE).transpose(1, 0, 2, 3)


def _ca_kernel(q_ref, k_ref, v_ref, bias_ref, qg_ref, kg_ref, o_ref, kn_ref, *, scale):
    kn_ref[...] = _rms(k_ref[...].astype(F32), kg_ref[...]).astype(BF16)
    n_blocks = q_ref.shape[0] // CA_QBLOCK

    def body(i, carry):
        r0 = pl.multiple_of(i * CA_QBLOCK, CA_QBLOCK)
        w0 = pl.multiple_of(jnp.maximum(i - CA_LEAD, 0) * CA_QBLOCK, CA_QBLOCK)
        q = (_rms(q_ref[pl.ds(r0, CA_QBLOCK), :].astype(F32), qg_ref[...]) * scale).astype(BF16)
        s = lax.dot_general(q, kn_ref[pl.ds(w0, CA_WINDOW), :], (((1,), (1,)), ((), ())),
                            preferred_element_type=F32)
        s = s + bias_ref[jnp.minimum(i, CA_LEAD), 0]
        p = jnp.exp2(s - jnp.max(s, axis=-1, keepdims=True))
        denom = jnp.sum(p, axis=-1, keepdims=True)
        out = jnp.dot(p.astype(BF16), v_ref[pl.ds(w0, CA_WINDOW), :], preferred_element_type=F32)
        o_ref[pl.ds(r0, CA_QBLOCK), :] = (out / denom).astype(o_ref.dtype)
        return carry

    lax.fori_loop(0, n_blocks, body, 0, unroll=4)


def chunk_band_attention(proj, q_col, k_col, v_col, heads, bsz, seq, q_gain, k_gain, rel_bias):
    bias = _ca_bias(rel_bias)
    head_block = lambda col: pl.BlockSpec((seq, HEAD_DIM), lambda h, b: (b, col + h))
    gain = pl.BlockSpec((1, HEAD_DIM), lambda h, b: (0, 0))
    return pl.pallas_call(
        functools.partial(_ca_kernel, scale=LOG2E * HEAD_DIM ** -0.5),
        grid=(heads, bsz),
        in_specs=[head_block(q_col), head_block(k_col), head_block(v_col),
                  pl.BlockSpec((CA_LEAD + 1, 1, CA_QBLOCK, CA_WINDOW), lambda h, b: (0, h, 0, 0)),
                  gain, gain],
        out_specs=pl.BlockSpec((seq, HEAD_DIM), lambda h, b: (b, h)),
        out_shape=jax.ShapeDtypeStruct((bsz * seq, heads * HEAD_DIM), BF16),
        scratch_shapes=[pltpu.VMEM((seq, HEAD_DIM), BF16)],
        compiler_params=_params("parallel", "parallel"),
    )(proj, proj, proj, bias, q_gain.reshape(1, HEAD_DIM), k_gain.reshape(1, HEAD_DIM))


DA_BLOCK = 512


def _da_kernel(q1_ref, q2_ref, k1_ref, k2_ref, v_ref, qg_ref, kg_ref, lam_ref, slope_ref, sg_ref,
               o_ref, kn_ref, *, scale, lambda_init):
    tq = DA_BLOCK
    n_blocks = q1_ref.shape[0] // tq
    for c, k_ref in enumerate((k1_ref, k2_ref)):
        kn_ref[c] = _rms(k_ref[...].astype(F32), kg_ref[...]).astype(BF16)
    lam = (jnp.exp(jnp.sum(lam_ref[0:1, :] * lam_ref[1:2, :], axis=-1, keepdims=True))
           - jnp.exp(jnp.sum(lam_ref[2:3, :] * lam_ref[3:4, :], axis=-1, keepdims=True))
           + lambda_init)
    slope = slope_ref[0, :, 0:1]
    row = lax.broadcasted_iota(jnp.int32, (tq, tq), 0)
    col = lax.broadcasted_iota(jnp.int32, (tq, tq), 1)
    delta = (row - col).astype(F32)
    alibi_past = slope * delta
    alibi_diag = slope * jnp.abs(delta)
    allowed = (col // CHUNK) <= (row // CHUNK)

    def update(carry, s, v, shift):
        m, l, acc = carry
        m_new = jnp.maximum(m, jnp.max(s, axis=-1, keepdims=True) - shift)
        alpha = jnp.exp2(m - m_new)
        p = jnp.exp2(s - (m_new + shift))
        l = alpha * l + jnp.sum(p, axis=-1, keepdims=True)
        acc = alpha * acc + jnp.dot(p.astype(BF16), v, preferred_element_type=F32)
        return m_new, l, acc

    def scores(q, c, c0):
        return lax.dot_general(q, kn_ref[c, pl.ds(c0, tq), :], (((1,), (1,)), ((), ())),
                               preferred_element_type=F32)

    def q_block(i, carry):
        r0 = pl.multiple_of(i * tq, tq)
        qs = [(_rms(q_ref[pl.ds(r0, tq), :].astype(F32), qg_ref[...]) * scale).astype(BF16)
              for q_ref in (q1_ref, q2_ref)]

        def k_block(j, states):
            c0 = pl.multiple_of(j * tq, tq)
            gap = slope * (jnp.zeros((1, 1), jnp.int32) + (i - j) * tq).astype(F32)
            v = v_ref[pl.ds(c0, tq), :]
            return tuple(update(states[c], scores(qs[c], c, c0) - alibi_past, v, gap)
                         for c in range(2))

        init = (jnp.full((tq, 1), MASK_VALUE, F32), jnp.zeros((tq, 1), F32),
                jnp.zeros((tq, v_ref.shape[1]), F32))
        states = lax.fori_loop(0, i, k_block, (init, init))
        v = v_ref[pl.ds(r0, tq), :]
        outs = []
        for c in range(2):
            s = jnp.where(allowed, scores(qs[c], c, r0) - alibi_diag, MASK_VALUE)
            _, l, acc = update(states[c], s, v, 0.0)
            outs.append(acc / l)
        out = outs[0] - lam * outs[1]
        out = _rms(out, sg_ref[...]) * (1.0 - lambda_init)
        o_ref[pl.ds(r0, tq), :] = out.astype(o_ref.dtype)
        return carry

    lax.fori_loop(0, n_blocks, q_block, 0)


def diff_attention(proj, q_col, k_col, v_col, heads, bsz, seq, q_gain, k_gain, lam_vecs, subln_gain,
                   lambda_init):
    slopes = LOG2E * 2.0 ** (-8.0 * jnp.arange(1, heads + 1, dtype=F32) / heads)
    slopes = jnp.broadcast_to(slopes[:, None, None], (heads, 1, HEAD_DIM))
    qk_block = lambda col: pl.BlockSpec((seq, HEAD_DIM), lambda b, h: (b, col + 2 * h))
    vec = lambda n: pl.BlockSpec((1, n), lambda b, h: (0, 0))
    return pl.pallas_call(
        functools.partial(_da_kernel, scale=LOG2E * HEAD_DIM ** -0.5, lambda_init=lambda_init),
        grid=(bsz, heads),
        in_specs=[qk_block(q_col), qk_block(q_col + 1), qk_block(k_col), qk_block(k_col + 1),
                  pl.BlockSpec((seq, 2 * HEAD_DIM), lambda b, h: (b, v_col + h)),
                  vec(HEAD_DIM), vec(HEAD_DIM),
                  pl.BlockSpec((4, HEAD_DIM), lambda b, h: (0, 0)),
                  pl.BlockSpec((1, 1, HEAD_DIM), lambda b, h: (h, 0, 0)),
                  vec(2 * HEAD_DIM)],
        out_specs=pl.BlockSpec((seq, 2 * HEAD_DIM), lambda b, h: (b, h)),
        out_shape=jax.ShapeDtypeStruct((bsz * seq, heads * 2 * HEAD_DIM), BF16),
        scratch_shapes=[pltpu.VMEM((2, seq, HEAD_DIM), BF16)],
        compiler_params=_params("parallel", "parallel"),
    )(proj, proj, proj, proj, proj, q_gain.reshape(1, HEAD_DIM), k_gain.reshape(1, HEAD_DIM),
      lam_vecs, slopes, subln_gain.reshape(1, 2 * HEAD_DIM))


def _merge_kernel(ya_ref, yb_ref, yc_ref, wa_ref, wb_ref, wc_ref, ga_ref, gb_ref, gc_ref, o_ref):
    def branch(y_ref, w_ref, g_ref):
        out = jnp.dot(y_ref[...], w_ref[...], preferred_element_type=F32)
        return _sigmoid(g_ref[...].astype(F32)) * out

    merged = (branch(ya_ref, wa_ref, ga_ref) + branch(yb_ref, wb_ref, gb_ref)
              + branch(yc_ref, wc_ref, gc_ref))
    o_ref[...] = merged.astype(o_ref.dtype)


def gated_merge(ys, ws, layer, proj, gate_col, *, tm=1024, tn=1024):
    t = ys[0].shape[0]
    d = ws[0].shape[2]
    tm, tn = _tile(t, tm, align=8), _tile(math.gcd(d, gate_col), tn)
    y_spec = lambda y: pl.BlockSpec((tm, y.shape[1]), lambda j, i: (i, 0))
    w_spec = lambda w: pl.BlockSpec((None, w.shape[1], tn), lambda j, i: (layer, 0, j))
    g_spec = lambda b: pl.BlockSpec((tm, tn), lambda j, i: (i, (gate_col + b * d) // tn + j))
    return pl.pallas_call(
        _merge_kernel,
        grid=(d // tn, t // tm),
        in_specs=[y_spec(y) for y in ys] + [w_spec(w) for w in ws] + [g_spec(b) for b in range(3)],
        out_specs=pl.BlockSpec((tm, tn), lambda j, i: (i, j)),
        out_shape=jax.ShapeDtypeStruct((t, d), BF16),
        compiler_params=_params("parallel", "parallel"),
    )(*ys, *ws, proj, proj, proj)


def kernel(x, norm_mix, w_in, ssm_a_re, ssm_a_im, ssm_log_dt, ssm_b_re, ssm_b_im, ssm_c_re, ssm_c_im, ssm_d, ssm_w_glu, ssm_b_glu, ca_q_gain, ca_k_gain, ca_rel_bias, da_q_gain, da_k_gain, da_lam_q1, da_lam_k1, da_lam_q2, da_lam_k2, da_subln_gain, w_out_a, w_out_b, w_out_c, w_o, norm_mlp, w_ff1, w_ff2):
    bsz, seq, d_model = x.shape
    depth = w_in.shape[0]
    ssm_w, ca_w, da_w = w_out_a.shape[1], w_out_b.shape[1], w_out_c.shape[1]
    ca_heads, da_heads = ca_w // HEAD_DIM, da_w // (2 * HEAD_DIM)
    n_chunks = seq // SSM_CHUNK
    ca_q = ssm_w // HEAD_DIM
    ca_k, ca_v = ca_q + ca_heads, ca_q + 2 * ca_heads
    da_q = ca_q + 3 * ca_heads
    da_k, da_v = da_q + 2 * da_heads, da_q + 4 * da_heads
    gate_col = ssm_w + 3 * ca_w + 3 * da_w

    w_in, w_glu, w_o, w_ff1, w_ff2 = (w.astype(BF16) for w in (w_in, ssm_w_glu, w_o, w_ff1, w_ff2))
    w_out = tuple(w.astype(BF16) for w in (w_out_a, w_out_b, w_out_c))

    xt = x.reshape(bsz * seq, d_model)
    for l in range(depth):
        lambda_init = 0.8 - 0.6 * math.exp(-0.3 * l)
        h = rmsnorm(xt, norm_mix[l])
        proj = matmul(h, w_in, l, out_dtype=BF16)

        mats = _s5_matrices(ssm_a_re[l], ssm_a_im[l], ssm_log_dt[l], ssm_b_re[l], ssm_b_im[l],
                            ssm_c_re[l], ssm_c_im[l], n_chunks)
        y_s5 = s5_scan(proj, mats, bsz, seq, ssm_w)
        y_a = s5_glu(y_s5, proj, ssm_d[l], w_glu, l, ssm_b_glu[l])

        y_b = chunk_band_attention(proj, ca_q, ca_k, ca_v, ca_heads, bsz, seq,
                                   ca_q_gain[l], ca_k_gain[l], ca_rel_bias[l])
        lam_vecs = jnp.stack([da_lam_q1[l], da_lam_k1[l], da_lam_q2[l], da_lam_k2[l]]).astype(F32)
        y_c = diff_attention(proj, da_q, da_k, da_v // 2, da_heads, bsz, seq,
                             da_q_gain[l], da_k_gain[l], lam_vecs, da_subln_gain[l], lambda_init)

        merged = gated_merge((y_a, y_b, y_c), w_out, l, proj, gate_col)
        xt = matmul(merged, w_o, l, out_dtype=F32, epilogue="residual", residual=xt)
        h = rmsnorm(xt, norm_mlp[l])
        ff = matmul(h, w_ff1, l, out_dtype=BF16, epilogue="relu2")
        xt = matmul(ff, w_ff2, l, out_dtype=F32, epilogue="residual", residual=xt, tk=4096)
    return xt.reshape(bsz, seq, d_model)
```

```python
import functools
import math

import jax
import jax.numpy as jnp
from jax import lax
from jax.experimental import pallas as pl
from jax.experimental.pallas import tpu as pltpu

F32 = jnp.float32
BF16 = jnp.bfloat16

EPS = 1e-6
HEAD_DIM = 128
CHUNK = 64
SSM_GROUP = 16
SSM_CHUNK = 16
S5_LANES = 128
CA_LEFT_CHUNKS = 8
REL_CLIP = 128
MASK_VALUE = -1e30
LOG2E = math.log2(math.e)

V7X_VMEM_BYTES = 64 * 1024 * 1024
VMEM_LIMIT = V7X_VMEM_BYTES - 4 * 1024 * 1024


def _params(*semantics):
    return pltpu.CompilerParams(dimension_semantics=semantics, vmem_limit_bytes=VMEM_LIMIT)


def _rms(x, gain):
    return x * lax.rsqrt(jnp.mean(x * x, axis=-1, keepdims=True) + EPS) * gain


def _sigmoid(x):
    return 1.0 / (1.0 + jnp.exp(-x))


def _tile(dim, preferred, align=128):
    best = None
    for t in range(align, min(dim, preferred) + 1, align):
        if dim % t == 0:
            best = t
    assert best is not None, (dim, preferred)
    return best


def _rmsnorm_kernel(x_ref, g_ref, o_ref):
    o_ref[...] = _rms(x_ref[...], g_ref[...]).astype(o_ref.dtype)


def rmsnorm(x, gain, *, tm=256):
    t, d = x.shape
    tm = _tile(t, tm, align=8)
    return pl.pallas_call(
        _rmsnorm_kernel,
        grid=(t // tm,),
        in_specs=[pl.BlockSpec((tm, d), lambda i: (i, 0)),
                  pl.BlockSpec((1, d), lambda i: (0, 0))],
        out_specs=pl.BlockSpec((tm, d), lambda i: (i, 0)),
        out_shape=jax.ShapeDtypeStruct((t, d), BF16),
        compiler_params=_params("parallel"),
    )(x, gain.reshape(1, d))


def _matmul_kernel(*refs, epilogue, nk):
    if epilogue == "residual":
        a_ref, w_ref, r_ref, o_ref = refs
    else:
        a_ref, w_ref, o_ref = refs
    acc = jnp.dot(a_ref[...], w_ref[...].astype(BF16), preferred_element_type=F32)
    if nk == 1:
        if epilogue == "residual":
            acc = acc + r_ref[...]
        elif epilogue == "relu2":
            acc = jnp.square(jnp.maximum(acc, 0.0))
        o_ref[...] = acc.astype(o_ref.dtype)
    else:
        k = pl.program_id(2)

        @pl.when(k == 0)
        def _():
            o_ref[...] = acc + r_ref[...]

        @pl.when(k > 0)
        def _():
            o_ref[...] += acc


def matmul(a, w, layer, *, out_dtype, epilogue="none", residual=None, tm=1024, tn=1024, tk=None,
           rows_outer=False):
    m, kdim = a.shape
    _, _, n = w.shape
    tm, tn = _tile(m, tm), _tile(n, tn)
    tk = kdim if tk is None else _tile(kdim, tk)
    nk = kdim // tk
    if nk > 1:
        assert epilogue == "residual" and out_dtype == F32
    if rows_outer:
        grid = (m // tm, n // tn, nk)
        ij = lambda g0, g1: (g0, g1)
        a_mode = dict(pipeline_mode=pl.Buffered(1)) if nk == 1 else {}
    else:
        grid = (n // tn, m // tm, nk)
        ij = lambda g0, g1: (g1, g0)
        a_mode = {}
    in_specs = [pl.BlockSpec((tm, tk), lambda g0, g1, k: (ij(g0, g1)[0], k), **a_mode),
                pl.BlockSpec((None, tk, tn), lambda g0, g1, k: (layer, k, ij(g0, g1)[1]))]
    args = [a, w]
    if epilogue == "residual":
        in_specs.append(pl.BlockSpec((tm, tn), lambda g0, g1, k: ij(g0, g1)))
        args.append(residual)
    return pl.pallas_call(
        functools.partial(_matmul_kernel, epilogue=epilogue, nk=nk),
        grid=grid,
        in_specs=in_specs,
        out_specs=pl.BlockSpec((tm, tn), lambda g0, g1, k: ij(g0, g1)),
        out_shape=jax.ShapeDtypeStruct((m, n), out_dtype),
        compiler_params=_params("parallel", "parallel", "arbitrary"),
    )(*args)


def _s5_matrices(a_re, a_im, log_dt, b_re, b_im, c_re, c_im, n_chunks):
    L = SSM_CHUNK
    g, p = a_re.shape
    h = b_re.shape[-1]
    gt = S5_LANES // h
    lt = g // gt

    def cmul(xr, xi, yr, yi):
        return xr * yr - xi * yi, xr * yi + xi * yr

    lam_r, lam_i = jnp.minimum(a_re.astype(F32), -1e-4), a_im.astype(F32)
    dt = jnp.exp(log_dt.astype(F32))[:, None]
    z_r, z_i = lam_r * dt, lam_i * dt

    def a_pow(n):
        mag, ang = jnp.exp(z_r[:, None, :] * n[None, :, None]), z_i[:, None, :] * n[None, :, None]
        return mag * jnp.cos(ang), mag * jnp.sin(ang)

    ab_r, ab_i = jnp.exp(z_r) * jnp.cos(z_i), jnp.exp(z_r) * jnp.sin(z_i)
    den = lam_r * lam_r + lam_i * lam_i
    f_r = ((ab_r - 1.0) * lam_r + ab_i * lam_i) / den
    f_i = (ab_i * lam_r - (ab_r - 1.0) * lam_i) / den
    bb_r, bb_i = cmul(f_r[:, None, :], f_i[:, None, :],
                      b_re.astype(F32).transpose(0, 2, 1), b_im.astype(F32).transpose(0, 2, 1))
    pw_r, pw_i = a_pow(jnp.arange(L + 1, dtype=F32))
    cr, ci = c_re.astype(F32)[:, None], c_im.astype(F32)[:, None]

    ca_r, ca_i = cmul(cr, ci, pw_r[:, :L, None, :], pw_i[:, :L, None, :])
    kern = jnp.einsum("glip,gjp->glji", ca_r, bb_r, precision=lax.Precision.HIGHEST)
    kern = kern - jnp.einsum("glip,gjp->glji", ca_i, bb_i, precision=lax.Precision.HIGHEST)
    ws_r, ws_i = cmul(pw_r[:, L - 1::-1, None, :], pw_i[:, L - 1::-1, None, :],
                      bb_r[:, None], bb_i[:, None])
    wc_r, wc_i = cmul(cr, ci, pw_r[:, 1:, None, :], pw_i[:, 1:, None, :])

    def tile_rows(a):
        n = a.shape[-1]
        return a.reshape(lt, gt, L, h, n).transpose(0, 2, 1, 3, 4).reshape(lt, L * gt * h, n)

    kern = tile_rows(kern)
    wsum = tile_rows(jnp.concatenate([ws_r, ws_i], axis=-1))
    wcar = tile_rows(jnp.concatenate([wc_r, -wc_i], axis=-1))
    n_steps = int(math.log2(n_chunks))
    ak_r, ak_i = a_pow(L * 2.0 ** jnp.arange(n_steps, dtype=F32))
    tile_cols = lambda a: a.reshape(lt, gt, n_steps, 2 * p).transpose(0, 2, 1, 3).reshape(
        lt, n_steps, gt * 2 * p)
    ar = tile_cols(jnp.concatenate([ak_r, ak_r], axis=-1))
    ai = tile_cols(jnp.concatenate([-ak_i, ak_i], axis=-1))
    return kern.astype(BF16), wsum.astype(BF16), wcar.astype(BF16), ar, ai


def _s5_kernel(u_ref, kern_ref, wsum_ref, wcar_ref, ar_ref, ai_ref, y_ref,
               x_ref, dcat_ref, wsum_full_ref, wcar_full_ref, *, row_tile):
    L = SSM_CHUNK
    seq, lanes = u_ref.shape
    n_chunks = seq // L
    h = kern_ref.shape[-1]
    gt = lanes // h
    state_w = wsum_ref.shape[-1]

    @pl.when(pl.program_id(1) == 0)
    def _():
        rows = kern_ref.shape[1]
        row_group = (lax.broadcasted_iota(jnp.int32, (rows, lanes), 0) // h) % gt
        lane = lax.broadcasted_iota(jnp.int32, (rows, lanes), 1)
        spread = (lax.broadcasted_iota(jnp.int32, (h, lanes), 1) % h
                  == lax.broadcasted_iota(jnp.int32, (h, lanes), 0)).astype(BF16)
        tiled = jnp.dot(kern_ref[0], spread, preferred_element_type=F32)
        dcat_ref[...] = jnp.where(row_group == lane // h, tiled, 0.0).astype(BF16)
        for grp in range(gt):
            cols = slice(grp * state_w, (grp + 1) * state_w)
            wsum_full_ref[:, cols] = jnp.where(row_group == grp, wsum_ref[0], 0.0)
            wcar_full_ref[:, cols] = jnp.where(row_group == grp, wcar_ref[0], 0.0)

    step = lax.broadcasted_iota(jnp.int32, (row_tile, lanes), 0) % L

    def intra(i, carry):
        r0 = pl.multiple_of(i * row_tile, row_tile)
        x = u_ref[pl.ds(r0, row_tile), :].astype(F32)
        x_ref[pl.ds(r0, row_tile), :] = x
        lagged = [x.astype(BF16)] + [
            jnp.where(step >= lag, pltpu.roll(x, lag, 0), 0.0).astype(BF16) for lag in range(1, L)]
        y_ref[pl.ds(r0, row_tile), :] = jnp.dot(jnp.concatenate(lagged, axis=1), dcat_ref[...],
                                                preferred_element_type=F32)
        return carry

    lax.fori_loop(0, seq // row_tile, intra, 0)

    steps = [x_ref[pl.ds(s, n_chunks, stride=L), :].astype(BF16) for s in range(L)]
    state = jnp.dot(jnp.concatenate(steps, axis=1), wsum_full_ref[...],
                    preferred_element_type=F32)
    chunk = lax.broadcasted_iota(jnp.int32, state.shape, 0)

    def swap_re_im(a):
        return jnp.concatenate(
            [pltpu.roll(a[:, grp * state_w:(grp + 1) * state_w], state_w // 2, 1)
             for grp in range(gt)], axis=1)

    for k in range(ar_ref.shape[1]):
        shift = 1 << k
        prev = jnp.where(chunk >= shift, pltpu.roll(state, shift, 0), 0.0)
        state = state + ar_ref[0, k:k + 1, :] * prev + ai_ref[0, k:k + 1, :] * swap_re_im(prev)
    incoming = jnp.where(chunk >= 1, pltpu.roll(state, 1, 0), 0.0)
    carried = lax.dot_general(incoming.astype(BF16), wcar_full_ref[...], (((1,), (1,)), ((), ())),
                              preferred_element_type=F32)
    for t in range(L):
        rows = pl.ds(t, n_chunks, stride=L)
        y_ref[rows, :] = y_ref[rows, :] + carried[:, t * lanes:(t + 1) * lanes]


def s5_scan(proj, mats, bsz, seq, width, *, row_tile=512):
    kern, wsum, wcar, ar, ai = mats
    rows, state_w = wsum.shape[1:]
    gt = S5_LANES // kern.shape[-1]
    assert state_w == S5_LANES
    blk = lambda a: pl.BlockSpec((1,) + a.shape[1:], lambda j, b: (j, 0, 0))
    return pl.pallas_call(
        functools.partial(_s5_kernel, row_tile=_tile(seq, row_tile)),
        grid=(width // S5_LANES, bsz),
        in_specs=[pl.BlockSpec((seq, S5_LANES), lambda j, b: (b, j)),
                  blk(kern), blk(wsum), blk(wcar), blk(ar), blk(ai)],
        out_specs=pl.BlockSpec((seq, S5_LANES), lambda j, b: (b, j)),
        out_shape=jax.ShapeDtypeStruct((bsz * seq, width), F32),
        scratch_shapes=[pltpu.VMEM((seq, S5_LANES), F32),
                        pltpu.VMEM((rows, S5_LANES), BF16),
                        pltpu.VMEM((rows, gt * state_w), BF16),
                        pltpu.VMEM((rows, gt * state_w), BF16)],
        compiler_params=_params("arbitrary", "arbitrary"),
    )(proj, kern, wsum, wcar, ar, ai)


def _s5_glu_kernel(y_ref, u_ref, d_ref, w_ref, b_ref, o_ref):
    y = y_ref[...] + d_ref[...] * u_ref[...].astype(F32)
    z = y * (0.5 * (1.0 + jnp.tanh(math.sqrt(2.0 / math.pi) * (y + 0.044715 * (y * y * y)))))
    gate = jnp.dot(z.astype(BF16), w_ref[...], preferred_element_type=F32) + b_ref[...]
    o_ref[...] = (z * _sigmoid(gate)).astype(o_ref.dtype)


def s5_glu(y, proj, d_skip, w_glu, layer, b_glu, *, tm=512):
    t, w = y.shape
    tm = _tile(t, tm, align=8)
    row = lambda a: pl.BlockSpec((1, w), lambda i: (0, 0))
    return pl.pallas_call(
        _s5_glu_kernel,
        grid=(t // tm,),
        in_specs=[pl.BlockSpec((tm, w), lambda i: (i, 0)),
                  pl.BlockSpec((tm, w), lambda i: (i, 0)),
                  row(d_skip),
                  pl.BlockSpec((None, w, w), lambda i: (layer, 0, 0)),
                  row(b_glu)],
        out_specs=pl.BlockSpec((tm, w), lambda i: (i, 0)),
        out_shape=jax.ShapeDtypeStruct((t, w), BF16),
        compiler_params=_params("parallel"),
    )(y, proj, d_skip.reshape(1, w), w_glu, b_glu.reshape(1, w))


CA_QBLOCK = 2 * CHUNK
CA_WINDOW = CA_LEFT_CHUNKS * CHUNK + CA_QBLOCK
CA_LEAD = CA_LEFT_CHUNKS * CHUNK // CA_QBLOCK


def _ca_bias(rel_bias):
    heads = rel_bias.shape[0]
    o = jnp.arange(CA_LEAD + 1)[:, None, None]
    q_pos = o * CA_QBLOCK + jnp.arange(CA_QBLOCK)[None, :, None]
    k_pos = jnp.arange(CA_WINDOW)[None, None, :]
    q_chunk, k_chunk = q_pos // CHUNK, k_pos // CHUNK
    allowed = (k_chunk <= q_chunk) & (k_chunk >= q_chunk - CA_LEFT_CHUNKS)
    period = CA_WINDOW + CA_QBLOCK
    n = jnp.arange(period)
    k_minus_q = jnp.where(n < CA_WINDOW, n, n - period)
    idx = jnp.clip(o[:, :, 0] * CA_QBLOCK - k_minus_q[None, :], -REL_CLIP, REL_CLIP) + REL_CLIP
    line = LOG2E * rel_bias.astype(F32)[:, idx]
    tiled = jnp.tile(line, (1, 1, CA_QBLOCK))[..., :CA_QBLOCK * (period - 1)]
    bias = tiled.reshape(heads, CA_LEAD + 1, CA_QBLOCK, period - 1)[..., :CA_WINDOW]
    return jnp.where(allowed[None], bias, MASK_VALUE).transpose(1, 0, 2, 3)


def _ca_kernel(q_ref, k_ref, v_ref, bias_ref, qg_ref, kg_ref, o_ref, kn_ref, *, scale):
    kn_ref[...] = _rms(k_ref[...].astype(F32), kg_ref[...]).astype(BF16)
    n_blocks = q_ref.shape[0] // CA_QBLOCK

    def body(i, carry):
        r0 = pl.multiple_of(i * CA_QBLOCK, CA_QBLOCK)
        w0 = pl.multiple_of(jnp.maximum(i - CA_LEAD, 0) * CA_QBLOCK, CA_QBLOCK)
        q = (_rms(q_ref[pl.ds(r0, CA_QBLOCK), :].astype(F32), qg_ref[...]) * scale).astype(BF16)
        s = lax.dot_general(q, kn_ref[pl.ds(w0, CA_WINDOW), :], (((1,), (1,)), ((), ())),
                            preferred_element_type=F32)
        s = s + bias_ref[jnp.minimum(i, CA_LEAD), 0]
        p = jnp.exp2(s - jnp.max(s, axis=-1, keepdims=True))
        denom = jnp.sum(p, axis=-1, keepdims=True)
        out = jnp.dot(p.astype(BF16), v_ref[pl.ds(w0, CA_WINDOW), :], preferred_element_type=F32)
        o_ref[pl.ds(r0, CA_QBLOCK), :] = (out / denom).astype(o_ref.dtype)
        return carry

    lax.fori_loop(0, n_blocks, body, 0, unroll=4)


def chunk_band_attention(proj, q_col, k_col, v_col, heads, bsz, seq, q_gain, k_gain, rel_bias):
    bias = _ca_bias(rel_bias)
    head_block = lambda col: pl.BlockSpec((seq, HEAD_DIM), lambda h, b: (b, col + h))
    gain = pl.BlockSpec((1, HEAD_DIM), lambda h, b: (0, 0))
    return pl.pallas_call(
        functools.partial(_ca_kernel, scale=LOG2E * HEAD_DIM ** -0.5),
        grid=(heads, bsz),
        in_specs=[head_block(q_col), head_block(k_col), head_block(v_col),
                  pl.BlockSpec((CA_LEAD + 1, 1, CA_QBLOCK, CA_WINDOW), lambda h, b: (0, h, 0, 0)),
                  gain, gain],
        out_specs=pl.BlockSpec((seq, HEAD_DIM), lambda h, b: (b, h)),
        out_shape=jax.ShapeDtypeStruct((bsz * seq, heads * HEAD_DIM), BF16),
        scratch_shapes=[pltpu.VMEM((seq, HEAD_DIM), BF16)],
        compiler_params=_params("parallel", "parallel"),
    )(proj, proj, proj, bias, q_gain.reshape(1, HEAD_DIM), k_gain.reshape(1, HEAD_DIM))


DA_BLOCK = 512


def _da_kernel(q1_ref, q2_ref, k1_ref, k2_ref, v_ref, qg_ref, kg_ref, lam_ref, slope_ref, sg_ref,
               o_ref, kn_ref, *, scale, lambda_init):
    tq = DA_BLOCK
    n_blocks = q1_ref.shape[0] // tq
    for c, k_ref in enumerate((k1_ref, k2_ref)):
        kn_ref[c] = _rms(k_ref[...].astype(F32), kg_ref[...]).astype(BF16)
    lam = (jnp.exp(jnp.sum(lam_ref[0:1, :] * lam_ref[1:2, :], axis=-1, keepdims=True))
           - jnp.exp(jnp.sum(lam_ref[2:3, :] * lam_ref[3:4, :], axis=-1, keepdims=True))
           + lambda_init)
    slope = slope_ref[0, :, 0:1]
    row = lax.broadcasted_iota(jnp.int32, (tq, tq), 0)
    col = lax.broadcasted_iota(jnp.int32, (tq, tq), 1)
    delta = (row - col).astype(F32)
    alibi_past = slope * delta
    alibi_diag = slope * jnp.abs(delta)
    allowed = (col // CHUNK) <= (row // CHUNK)

    def update(carry, s, v, shift):
        m, l, acc = carry
        m_new = jnp.maximum(m, jnp.max(s, axis=-1, keepdims=True) - shift)
        alpha = jnp.exp2(m - m_new)
        p = jnp.exp2(s - (m_new + shift))
        l = alpha * l + jnp.sum(p, axis=-1, keepdims=True)
        acc = alpha * acc + jnp.dot(p.astype(BF16), v, preferred_element_type=F32)
        return m_new, l, acc

    def scores(q, c, c0):
        return lax.dot_general(q, kn_ref[c, pl.ds(c0, tq), :], (((1,), (1,)), ((), ())),
                               preferred_element_type=F32)

    def q_block(i, carry):
        r0 = pl.multiple_of(i * tq, tq)
        qs = [(_rms(q_ref[pl.ds(r0, tq), :].astype(F32), qg_ref[...]) * scale).astype(BF16)
              for q_ref in (q1_ref, q2_ref)]

        def k_block(j, states):
            c0 = pl.multiple_of(j * tq, tq)
            gap = slope * (jnp.zeros((1, 1), jnp.int32) + (i - j) * tq).astype(F32)
            v = v_ref[pl.ds(c0, tq), :]
            return tuple(update(states[c], scores(qs[c], c, c0) - alibi_past, v, gap)
                         for c in range(2))

        init = (jnp.full((tq, 1), MASK_VALUE, F32), jnp.zeros((tq, 1), F32),
                jnp.zeros((tq, v_ref.shape[1]), F32))
        states = lax.fori_loop(0, i, k_block, (init, init))
        v = v_ref[pl.ds(r0, tq), :]
        outs = []
        for c in range(2):
            s = jnp.where(allowed, scores(qs[c], c, r0) - alibi_diag, MASK_VALUE)
            _, l, acc = update(states[c], s, v, 0.0)
            outs.append(acc / l)
        out = outs[0] - lam * outs[1]
        out = _rms(out, sg_ref[...]) * (1.0 - lambda_init)
        o_ref[pl.ds(r0, tq), :] = out.astype(o_ref.dtype)
        return carry

    lax.fori_loop(0, n_blocks, q_block, 0)


def diff_attention(proj, q_col, k_col, v_col, heads, bsz, seq, q_gain, k_gain, lam_vecs, subln_gain,
                   lambda_init):
    slopes = LOG2E * 2.0 ** (-8.0 * jnp.arange(1, heads + 1, dtype=F32) / heads)
    slopes = jnp.broadcast_to(slopes[:, None, None], (heads, 1, HEAD_DIM))
    qk_block = lambda col: pl.BlockSpec((seq, HEAD_DIM), lambda b, h: (b, col + 2 * h))
    vec = lambda n: pl.BlockSpec((1, n), lambda b, h: (0, 0))
    return pl.pallas_call(
        functools.partial(_da_kernel, scale=LOG2E * HEAD_DIM ** -0.5, lambda_init=lambda_init),
        grid=(bsz, heads),
        in_specs=[qk_block(q_col), qk_block(q_col + 1), qk_block(k_col), qk_block(k_col + 1),
                  pl.BlockSpec((seq, 2 * HEAD_DIM), lambda b, h: (b, v_col + h)),
                  vec(HEAD_DIM), vec(HEAD_DIM),
                  pl.BlockSpec((4, HEAD_DIM), lambda b, h: (0, 0)),
                  pl.BlockSpec((1, 1, HEAD_DIM), lambda b, h: (h, 0, 0)),
                  vec(2 * HEAD_DIM)],
        out_specs=pl.BlockSpec((seq, 2 * HEAD_DIM), lambda b, h: (b, h)),
        out_shape=jax.ShapeDtypeStruct((bsz * seq, heads * 2 * HEAD_DIM), BF16),
        scratch_shapes=[pltpu.VMEM((2, seq, HEAD_DIM), BF16)],
        compiler_params=_params("parallel", "parallel"),
    )(proj, proj, proj, proj, proj, q_gain.reshape(1, HEAD_DIM), k_gain.reshape(1, HEAD_DIM),
      lam_vecs, slopes, subln_gain.reshape(1, 2 * HEAD_DIM))


def _merge_kernel(ya_ref, yb_ref, yc_ref, wa_ref, wb_ref, wc_ref, ga_ref, gb_ref, gc_ref, o_ref):
    def branch(y_ref, w_ref, g_ref):
        out = jnp.dot(y_ref[...], w_ref[...], preferred_element_type=F32)
        return _sigmoid(g_ref[...].astype(F32)) * out

    merged = (branch(ya_ref, wa_ref, ga_ref) + branch(yb_ref, wb_ref, gb_ref)
              + branch(yc_ref, wc_ref, gc_ref))
    o_ref[...] = merged.astype(o_ref.dtype)


def gated_merge(ys, ws, layer, proj, gate_col, *, tm=1024, tn=1024):
    t = ys[0].shape[0]
    d = ws[0].shape[2]
    tm, tn = _tile(t, tm, align=8), _tile(math.gcd(d, gate_col), tn)
    y_spec = lambda y: pl.BlockSpec((tm, y.shape[1]), lambda j, i: (i, 0))
    w_spec = lambda w: pl.BlockSpec((None, w.shape[1], tn), lambda j, i: (layer, 0, j))
    g_spec = lambda b: pl.BlockSpec((tm, tn), lambda j, i: (i, (gate_col + b * d) // tn + j))
    return pl.pallas_call(
        _merge_kernel,
        grid=(d // tn, t // tm),
        in_specs=[y_spec(y) for y in ys] + [w_spec(w) for w in ws] + [g_spec(b) for b in range(3)],
        out_specs=pl.BlockSpec((tm, tn), lambda j, i: (i, j)),
        out_shape=jax.ShapeDtypeStruct((t, d), BF16),
        compiler_params=_params("parallel", "parallel"),
    )(*ys, *ws, proj, proj, proj)


def kernel(x, norm_mix, w_in, ssm_a_re, ssm_a_im, ssm_log_dt, ssm_b_re, ssm_b_im, ssm_c_re, ssm_c_im, ssm_d, ssm_w_glu, ssm_b_glu, ca_q_gain, ca_k_gain, ca_rel_bias, da_q_gain, da_k_gain, da_lam_q1, da_lam_k1, da_lam_q2, da_lam_k2, da_subln_gain, w_out_a, w_out_b, w_out_c, w_o, norm_mlp, w_ff1, w_ff2):
    bsz, seq, d_model = x.shape
    depth = w_in.shape[0]
    ssm_w, ca_w, da_w = w_out_a.shape[1], w_out_b.shape[1], w_out_c.shape[1]
    ca_heads, da_heads = ca_w // HEAD_DIM, da_w // (2 * HEAD_DIM)
    n_chunks = seq // SSM_CHUNK
    ca_q = ssm_w // HEAD_DIM
    ca_k, ca_v = ca_q + ca_heads, ca_q + 2 * ca_heads
    da_q = ca_q + 3 * ca_heads
    da_k, da_v = da_q + 2 * da_heads, da_q + 4 * da_heads
    gate_col = ssm_w + 3 * ca_w + 3 * da_w

    w_glu, w_ff2 = ssm_w_glu.astype(BF16), w_ff2.astype(BF16)
    w_out = tuple(w.astype(BF16) for w in (w_out_a, w_out_b, w_out_c))
    stream_f32 = dict(rows_outer=True, tm=2048, tn=512)

    xt = x.reshape(bsz * seq, d_model)
    for l in range(depth):
        lambda_init = 0.8 - 0.6 * math.exp(-0.3 * l)
        h = rmsnorm(xt, norm_mix[l])
        proj = matmul(h, w_in, l, out_dtype=BF16, **stream_f32)

        mats = _s5_matrices(ssm_a_re[l], ssm_a_im[l], ssm_log_dt[l], ssm_b_re[l], ssm_b_im[l],
                            ssm_c_re[l], ssm_c_im[l], n_chunks)
        y_s5 = s5_scan(proj, mats, bsz, seq, ssm_w)
        y_a = s5_glu(y_s5, proj, ssm_d[l], w_glu, l, ssm_b_glu[l])

        y_b = chunk_band_attention(proj, ca_q, ca_k, ca_v, ca_heads, bsz, seq,
                                   ca_q_gain[l], ca_k_gain[l], ca_rel_bias[l])
        lam_vecs = jnp.stack([da_lam_q1[l], da_lam_k1[l], da_lam_q2[l], da_lam_k2[l]]).astype(F32)
        y_c = diff_attention(proj, da_q, da_k, da_v // 2, da_heads, bsz, seq,
                             da_q_gain[l], da_k_gain[l], lam_vecs, da_subln_gain[l], lambda_init)

        merged = gated_merge((y_a, y_b, y_c), w_out, l, proj, gate_col)
        xt = matmul(merged, w_o, l, out_dtype=F32, epilogue="residual", residual=xt,
                    rows_outer=True, tm=1024, tn=512)
        h = rmsnorm(xt, norm_mlp[l])
        ff = matmul(h, w_ff1, l, out_dtype=BF16, epilogue="relu2", **stream_f32)
        xt = matmul(ff, w_ff2, l, out_dtype=F32, epilogue="residual", residual=xt, tk=4096)
    return xt.reshape(bsz, seq, d_model)
```

```python
import functools
import math

import jax
import jax.numpy as jnp
from jax import lax
from jax.experimental import pallas as pl
from jax.experimental.pallas import tpu as pltpu

F32 = jnp.float32
BF16 = jnp.bfloat16

EPS = 1e-6
HEAD_DIM = 128
CHUNK = 64
SSM_GROUP = 16
SSM_CHUNK = 16
S5_LANES = 128
CA_LEFT_CHUNKS = 8
REL_CLIP = 128
MASK_VALUE = -1e30
LOG2E = math.log2(math.e)

V7X_VMEM_BYTES = 64 * 1024 * 1024
VMEM_LIMIT = V7X_VMEM_BYTES - 4 * 1024 * 1024


def _params(*semantics):
    return pltpu.CompilerParams(dimension_semantics=semantics, vmem_limit_bytes=VMEM_LIMIT)


def _rms(x, gain):
    return x * lax.rsqrt(jnp.mean(x * x, axis=-1, keepdims=True) + EPS) * gain


def _sigmoid(x):
    return 1.0 / (1.0 + jnp.exp(-x))


def _tile(dim, preferred, align=128):
    best = None
    for t in range(align, min(dim, preferred) + 1, align):
        if dim % t == 0:
            best = t
    assert best is not None, (dim, preferred)
    return best


def _rmsnorm_kernel(x_ref, g_ref, o_ref):
    o_ref[...] = _rms(x_ref[...], g_ref[...]).astype(o_ref.dtype)


def rmsnorm(x, gain, *, tm=256):
    t, d = x.shape
    tm = _tile(t, tm, align=8)
    return pl.pallas_call(
        _rmsnorm_kernel,
        grid=(t // tm,),
        in_specs=[pl.BlockSpec((tm, d), lambda i: (i, 0)),
                  pl.BlockSpec((1, d), lambda i: (0, 0))],
        out_specs=pl.BlockSpec((tm, d), lambda i: (i, 0)),
        out_shape=jax.ShapeDtypeStruct((t, d), BF16),
        compiler_params=_params("parallel"),
    )(x, gain.reshape(1, d))


def _matmul_kernel(*refs, epilogue, nk):
    if epilogue == "residual":
        a_ref, w_ref, r_ref, o_ref = refs
    else:
        a_ref, w_ref, o_ref = refs
    acc = jnp.dot(a_ref[...], w_ref[...].astype(BF16), preferred_element_type=F32)
    if nk == 1:
        if epilogue == "residual":
            acc = acc + r_ref[...]
        elif epilogue == "relu2":
            acc = jnp.square(jnp.maximum(acc, 0.0))
        o_ref[...] = acc.astype(o_ref.dtype)
    else:
        k = pl.program_id(2)

        @pl.when(k == 0)
        def _():
            o_ref[...] = acc + r_ref[...]

        @pl.when(k > 0)
        def _():
            o_ref[...] += acc


def matmul(a, w, layer, *, out_dtype, epilogue="none", residual=None, tm=1024, tn=1024, tk=None,
           rows_outer=False):
    m, kdim = a.shape
    _, _, n = w.shape
    tm, tn = _tile(m, tm), _tile(n, tn)
    tk = kdim if tk is None else _tile(kdim, tk)
    nk = kdim // tk
    if nk > 1:
        assert epilogue == "residual" and out_dtype == F32
    if rows_outer:
        grid = (m // tm, n // tn, nk)
        ij = lambda g0, g1: (g0, g1)
        a_mode = dict(pipeline_mode=pl.Buffered(1)) if nk == 1 else {}
    else:
        grid = (n // tn, m // tm, nk)
        ij = lambda g0, g1: (g1, g0)
        a_mode = {}
    in_specs = [pl.BlockSpec((tm, tk), lambda g0, g1, k: (ij(g0, g1)[0], k), **a_mode),
                pl.BlockSpec((None, tk, tn), lambda g0, g1, k: (layer, k, ij(g0, g1)[1]))]
    args = [a, w]
    if epilogue == "residual":
        in_specs.append(pl.BlockSpec((tm, tn), lambda g0, g1, k: ij(g0, g1)))
        args.append(residual)
    return pl.pallas_call(
        functools.partial(_matmul_kernel, epilogue=epilogue, nk=nk),
        grid=grid,
        in_specs=in_specs,
        out_specs=pl.BlockSpec((tm, tn), lambda g0, g1, k: ij(g0, g1)),
        out_shape=jax.ShapeDtypeStruct((m, n), out_dtype),
        compiler_params=_params("parallel", "parallel", "arbitrary"),
    )(*args)


def _s5_matrices(a_re, a_im, log_dt, b_re, b_im, c_re, c_im, n_chunks):
    L = SSM_CHUNK
    g, p = a_re.shape
    h = b_re.shape[-1]
    gt = S5_LANES // h
    lt = g // gt

    def cmul(xr, xi, yr, yi):
        return xr * yr - xi * yi, xr * yi + xi * yr

    lam_r, lam_i = jnp.minimum(a_re.astype(F32), -1e-4), a_im.astype(F32)
    dt = jnp.exp(log_dt.astype(F32))[:, None]
    z_r, z_i = lam_r * dt, lam_i * dt

    def a_pow(n):
        mag, ang = jnp.exp(z_r[:, None, :] * n[None, :, None]), z_i[:, None, :] * n[None, :, None]
        return mag * jnp.cos(ang), mag * jnp.sin(ang)

    ab_r, ab_i = jnp.exp(z_r) * jnp.cos(z_i), jnp.exp(z_r) * jnp.sin(z_i)
    den = lam_r * lam_r + lam_i * lam_i
    f_r = ((ab_r - 1.0) * lam_r + ab_i * lam_i) / den
    f_i = (ab_i * lam_r - (ab_r - 1.0) * lam_i) / den
    bb_r, bb_i = cmul(f_r[:, None, :], f_i[:, None, :],
                      b_re.astype(F32).transpose(0, 2, 1), b_im.astype(F32).transpose(0, 2, 1))
    pw_r, pw_i = a_pow(jnp.arange(L + 1, dtype=F32))
    cr, ci = c_re.astype(F32)[:, None], c_im.astype(F32)[:, None]

    ca_r, ca_i = cmul(cr, ci, pw_r[:, :L, None, :], pw_i[:, :L, None, :])
    kern = jnp.einsum("glip,gjp->glji", ca_r, bb_r, precision=lax.Precision.HIGHEST)
    kern = kern - jnp.einsum("glip,gjp->glji", ca_i, bb_i, precision=lax.Precision.HIGHEST)
    ws_r, ws_i = cmul(pw_r[:, L - 1::-1, None, :], pw_i[:, L - 1::-1, None, :],
                      bb_r[:, None], bb_i[:, None])
    wc_r, wc_i = cmul(cr, ci, pw_r[:, 1:, None, :], pw_i[:, 1:, None, :])

    def tile_rows(a):
        n = a.shape[-1]
        return a.reshape(lt, gt, L, h, n).transpose(0, 2, 1, 3, 4).reshape(lt, L * gt * h, n)

    kern = tile_rows(kern)
    wsum = tile_rows(jnp.concatenate([ws_r, ws_i], axis=-1))
    wcar = tile_rows(jnp.concatenate([wc_r, -wc_i], axis=-1))
    n_steps = int(math.log2(n_chunks))
    ak_r, ak_i = a_pow(L * 2.0 ** jnp.arange(n_steps, dtype=F32))
    tile_cols = lambda a: a.reshape(lt, gt, n_steps, 2 * p).transpose(0, 2, 1, 3).reshape(
        lt, n_steps, gt * 2 * p)
    ar = tile_cols(jnp.concatenate([ak_r, ak_r], axis=-1))
    ai = tile_cols(jnp.concatenate([-ak_i, ak_i], axis=-1))
    return kern.astype(BF16), wsum.astype(BF16), wcar.astype(BF16), ar, ai


def _s5_kernel(u_ref, kern_ref, wsum_ref, wcar_ref, ar_ref, ai_ref, y_ref,
               x_ref, dcat_ref, wsum_full_ref, wcar_full_ref, *, row_tile):
    L = SSM_CHUNK
    seq, lanes = u_ref.shape
    n_chunks = seq // L
    h = kern_ref.shape[-1]
    gt = lanes // h
    state_w = wsum_ref.shape[-1]

    @pl.when(pl.program_id(1) == 0)
    def _():
        rows = kern_ref.shape[1]
        row_group = (lax.broadcasted_iota(jnp.int32, (rows, lanes), 0) // h) % gt
        lane = lax.broadcasted_iota(jnp.int32, (rows, lanes), 1)
        spread = (lax.broadcasted_iota(jnp.int32, (h, lanes), 1) % h
                  == lax.broadcasted_iota(jnp.int32, (h, lanes), 0)).astype(BF16)
        tiled = jnp.dot(kern_ref[0], spread, preferred_element_type=F32)
        dcat_ref[...] = jnp.where(row_group == lane // h, tiled, 0.0).astype(BF16)
        for grp in range(gt):
            cols = slice(grp * state_w, (grp + 1) * state_w)
            wsum_full_ref[:, cols] = jnp.where(row_group == grp, wsum_ref[0], 0.0)
            wcar_full_ref[:, cols] = jnp.where(row_group == grp, wcar_ref[0], 0.0)

    step = lax.broadcasted_iota(jnp.int32, (row_tile, lanes), 0) % L

    def intra(i, carry):
        r0 = pl.multiple_of(i * row_tile, row_tile)
        x = u_ref[pl.ds(r0, row_tile), :].astype(F32)
        x_ref[pl.ds(r0, row_tile), :] = x
        lagged = [x.astype(BF16)] + [
            jnp.where(step >= lag, pltpu.roll(x, lag, 0), 0.0).astype(BF16) for lag in range(1, L)]
        y_ref[pl.ds(r0, row_tile), :] = jnp.dot(jnp.concatenate(lagged, axis=1), dcat_ref[...],
                                                preferred_element_type=F32)
        return carry

    lax.fori_loop(0, seq // row_tile, intra, 0)

    steps = [x_ref[pl.ds(s, n_chunks, stride=L), :].astype(BF16) for s in range(L)]
    state = jnp.dot(jnp.concatenate(steps, axis=1), wsum_full_ref[...],
                    preferred_element_type=F32)
    chunk = lax.broadcasted_iota(jnp.int32, state.shape, 0)

    def swap_re_im(a):
        return jnp.concatenate(
            [pltpu.roll(a[:, grp * state_w:(grp + 1) * state_w], state_w // 2, 1)
             for grp in range(gt)], axis=1)

    for k in range(ar_ref.shape[1]):
        shift = 1 << k
        prev = jnp.where(chunk >= shift, pltpu.roll(state, shift, 0), 0.0)
        state = state + ar_ref[0, k:k + 1, :] * prev + ai_ref[0, k:k + 1, :] * swap_re_im(prev)
    incoming = jnp.where(chunk >= 1, pltpu.roll(state, 1, 0), 0.0)
    carried = lax.dot_general(incoming.astype(BF16), wcar_full_ref[...], (((1,), (1,)), ((), ())),
                              preferred_element_type=F32)
    for t in range(L):
        rows = pl.ds(t, n_chunks, stride=L)
        y_ref[rows, :] = y_ref[rows, :] + carried[:, t * lanes:(t + 1) * lanes]


def s5_scan(proj, mats, bsz, seq, width, *, row_tile=512):
    kern, wsum, wcar, ar, ai = mats
    rows, state_w = wsum.shape[1:]
    gt = S5_LANES // kern.shape[-1]
    assert state_w == S5_LANES
    blk = lambda a: pl.BlockSpec((1,) + a.shape[1:], lambda j, b: (j, 0, 0))
    return pl.pallas_call(
        functools.partial(_s5_kernel, row_tile=_tile(seq, row_tile)),
        grid=(width // S5_LANES, bsz),
        in_specs=[pl.BlockSpec((seq, S5_LANES), lambda j, b: (b, j)),
                  blk(kern), blk(wsum), blk(wcar), blk(ar), blk(ai)],
        out_specs=pl.BlockSpec((seq, S5_LANES), lambda j, b: (b, j)),
        out_shape=jax.ShapeDtypeStruct((bsz * seq, width), F32),
        scratch_shapes=[pltpu.VMEM((seq, S5_LANES), F32),
                        pltpu.VMEM((rows, S5_LANES), BF16),
                        pltpu.VMEM((rows, gt * state_w), BF16),
                        pltpu.VMEM((rows, gt * state_w), BF16)],
        compiler_params=_params("arbitrary", "arbitrary"),
    )(proj, kern, wsum, wcar, ar, ai)


def _s5_glu_kernel(y_ref, u_ref, d_ref, w_ref, b_ref, o_ref):
    y = y_ref[...] + d_ref[...] * u_ref[...].astype(F32)
    z = y * (0.5 * (1.0 + jnp.tanh(math.sqrt(2.0 / math.pi) * (y + 0.044715 * (y * y * y)))))
    gate = jnp.dot(z.astype(BF16), w_ref[...], preferred_element_type=F32) + b_ref[...]
    o_ref[...] = (z * _sigmoid(gate)).astype(o_ref.dtype)


def s5_glu(y, proj, d_skip, w_glu, layer, b_glu, *, tm=512):
    t, w = y.shape
    tm = _tile(t, tm, align=8)
    row = lambda a: pl.BlockSpec((1, w), lambda i: (0, 0))
    return pl.pallas_call(
        _s5_glu_kernel,
        grid=(t // tm,),
        in_specs=[pl.BlockSpec((tm, w), lambda i: (i, 0)),
                  pl.BlockSpec((tm, w), lambda i: (i, 0)),
                  row(d_skip),
                  pl.BlockSpec((None, w, w), lambda i: (layer, 0, 0)),
                  row(b_glu)],
        out_specs=pl.BlockSpec((tm, w), lambda i: (i, 0)),
        out_shape=jax.ShapeDtypeStruct((t, w), BF16),
        compiler_params=_params("parallel"),
    )(y, proj, d_skip.reshape(1, w), w_glu, b_glu.reshape(1, w))


CA_QBLOCK = 2 * CHUNK
CA_WINDOW = CA_LEFT_CHUNKS * CHUNK + CA_QBLOCK
CA_LEAD = CA_LEFT_CHUNKS * CHUNK // CA_QBLOCK


def _ca_bias(rel_bias):
    heads = rel_bias.shape[0]
    o = jnp.arange(CA_LEAD + 1)[:, None, None]
    q_pos = o * CA_QBLOCK + jnp.arange(CA_QBLOCK)[None, :, None]
    k_pos = jnp.arange(CA_WINDOW)[None, None, :]
    q_chunk, k_chunk = q_pos // CHUNK, k_pos // CHUNK
    allowed = (k_chunk <= q_chunk) & (k_chunk >= q_chunk - CA_LEFT_CHUNKS)
    period = CA_WINDOW + CA_QBLOCK
    n = jnp.arange(period)
    k_minus_q = jnp.where(n < CA_WINDOW, n, n - period)
    idx = jnp.clip(o[:, :, 0] * CA_QBLOCK - k_minus_q[None, :], -REL_CLIP, REL_CLIP) + REL_CLIP
    line = LOG2E * rel_bias.astype(F32)[:, idx]
    tiled = jnp.tile(line, (1, 1, CA_QBLOCK))[..., :CA_QBLOCK * (period - 1)]
    bias = tiled.reshape(heads, CA_LEAD + 1, CA_QBLOCK, period - 1)[..., :CA_WINDOW]
    return jnp.where(allowed[None], bias, MASK_VALUE).transpose(1, 0, 2, 3)


def _ca_kernel(q_ref, k_ref, v_ref, bias_ref, qg_ref, kg_ref, o_ref, kn_ref, *, scale):
    kn_ref[...] = _rms(k_ref[...].astype(F32), kg_ref[...]).astype(BF16)
    n_blocks = q_ref.shape[0] // CA_QBLOCK

    def body(i, carry):
        r0 = pl.multiple_of(i * CA_QBLOCK, CA_QBLOCK)
        w0 = pl.multiple_of(jnp.maximum(i - CA_LEAD, 0) * CA_QBLOCK, CA_QBLOCK)
        q = (_rms(q_ref[pl.ds(r0, CA_QBLOCK), :].astype(F32), qg_ref[...]) * scale).astype(BF16)
        s = lax.dot_general(q, kn_ref[pl.ds(w0, CA_WINDOW), :], (((1,), (1,)), ((), ())),
                            preferred_element_type=F32)
        s = s + bias_ref[jnp.minimum(i, CA_LEAD), 0]
        p = jnp.exp2(s - jnp.max(s, axis=-1, keepdims=True))
        denom = jnp.sum(p, axis=-1, keepdims=True)
        out = jnp.dot(p.astype(BF16), v_ref[pl.ds(w0, CA_WINDOW), :], preferred_element_type=F32)
        o_ref[pl.ds(r0, CA_QBLOCK), :] = (out / denom).astype(o_ref.dtype)
        return carry

    lax.fori_loop(0, n_blocks, body, 0, unroll=4)


def chunk_band_attention(proj, q_col, k_col, v_col, heads, bsz, seq, q_gain, k_gain, rel_bias):
    bias = _ca_bias(rel_bias)
    head_block = lambda col: pl.BlockSpec((seq, HEAD_DIM), lambda h, b: (b, col + h))
    gain = pl.BlockSpec((1, HEAD_DIM), lambda h, b: (0, 0))
    return pl.pallas_call(
        functools.partial(_ca_kernel, scale=LOG2E * HEAD_DIM ** -0.5),
        grid=(heads, bsz),
        in_specs=[head_block(q_col), head_block(k_col), head_block(v_col),
                  pl.BlockSpec((CA_LEAD + 1, 1, CA_QBLOCK, CA_WINDOW), lambda h, b: (0, h, 0, 0)),
                  gain, gain],
        out_specs=pl.BlockSpec((seq, HEAD_DIM), lambda h, b: (b, h)),
        out_shape=jax.ShapeDtypeStruct((bsz * seq, heads * HEAD_DIM), BF16),
        scratch_shapes=[pltpu.VMEM((seq, HEAD_DIM), BF16)],
        compiler_params=_params("parallel", "parallel"),
    )(proj, proj, proj, bias, q_gain.reshape(1, HEAD_DIM), k_gain.reshape(1, HEAD_DIM))


DA_BLOCK = 512


def _split3(x):
    hi = x.astype(BF16)
    rest = x - hi.astype(F32)
    mid = rest.astype(BF16)
    return hi, mid, (rest - mid.astype(F32)).astype(BF16)


def _da_kernel(q1_ref, q2_ref, k1_ref, k2_ref, v_ref, qg_ref, kg_ref, lam_ref, slope_ref, sg_ref,
               o_ref, kx_ref, *, scale, lambda_init):
    tq = DA_BLOCK
    seq = q1_ref.shape[0]
    n_blocks = seq // tq
    slope = slope_ref[0, :, 0:1]
    lam = (jnp.exp(jnp.sum(lam_ref[0:1, :] * lam_ref[1:2, :], axis=-1, keepdims=True))
           - jnp.exp(jnp.sum(lam_ref[2:3, :] * lam_ref[3:4, :], axis=-1, keepdims=True))
           + lambda_init)

    def extra_columns(shape, first, second):
        lane = lax.broadcasted_iota(jnp.int32, shape, 1)
        out = jnp.zeros(shape, F32)
        for offset, pieces in ((0, first), (3, second)):
            for n, piece in enumerate(pieces):
                piece = piece if isinstance(piece, float) else piece.astype(F32)
                out = jnp.where(lane == offset + n, piece, out)
        return out.astype(BF16)

    ones = (1.0, 1.0, 1.0)
    key_pos = lax.broadcasted_iota(jnp.int32, (seq, 1), 0).astype(F32)
    key_extra = extra_columns((seq, HEAD_DIM), _split3(slope * key_pos), ones)
    for c, k_ref in enumerate((k1_ref, k2_ref)):
        kx_ref[c, :, 0:HEAD_DIM] = _rms(k_ref[...].astype(F32), kg_ref[...]).astype(BF16)
        kx_ref[c, :, HEAD_DIM:] = key_extra

    row = lax.broadcasted_iota(jnp.int32, (tq, tq), 0)
    col = lax.broadcasted_iota(jnp.int32, (tq, tq), 1)
    ahead = jnp.where(col > row, (2.0 * slope) * (row - col).astype(F32), 0.0)
    allowed = (col // CHUNK) <= (row // CHUNK)

    def scores(qx, c, c0):
        return lax.dot_general(qx, kx_ref[c, pl.ds(c0, tq), :], (((1,), (1,)), ((), ())),
                               preferred_element_type=F32)

    def fold(a, op):
        parts = [a[:, n * HEAD_DIM:(n + 1) * HEAD_DIM] for n in range(tq // HEAD_DIM)]
        while len(parts) > 1:
            parts = [op(parts[n], parts[n + 1]) for n in range(0, len(parts), 2)]
        return parts[0]

    def q_block(i, carry):
        r0 = pl.multiple_of(i * tq, tq)
        qn = [(_rms(q_ref[pl.ds(r0, tq), :].astype(F32), qg_ref[...]) * scale).astype(BF16)
              for q_ref in (q1_ref, q2_ref)]
        zeros3 = (0.0, 0.0, 0.0)
        qx = [jnp.concatenate([q, extra_columns((tq, HEAD_DIM), ones, zeros3)], axis=1) for q in qn]

        def max_step(j, mx):
            c0 = pl.multiple_of(j * tq, tq)
            return tuple(jnp.maximum(mx[c], fold(scores(qx[c], c, c0), jnp.maximum))
                         for c in range(2))

        start = jnp.full((tq, HEAD_DIM), MASK_VALUE, F32)
        mx = lax.fori_loop(0, i, max_step, (start, start))
        row_max = [jnp.maximum(
            jnp.max(mx[c], axis=-1, keepdims=True),
            jnp.max(jnp.where(allowed, scores(qx[c], c, r0) + ahead, MASK_VALUE),
                    axis=-1, keepdims=True)) for c in range(2)]
        qx = [jnp.concatenate([qn[c], extra_columns((tq, HEAD_DIM), ones, _split3(-row_max[c]))],
                              axis=1) for c in range(2)]

        def sum_step(j, st):
            c0 = pl.multiple_of(j * tq, tq)
            v = v_ref[pl.ds(c0, tq), :]
            out = []
            for c in range(2):
                p = jnp.exp2(scores(qx[c], c, c0))
                out.append((st[c][0] + fold(p, jnp.add),
                            st[c][1] + jnp.dot(p.astype(BF16), v, preferred_element_type=F32)))
            return tuple(out)

        init = (jnp.zeros((tq, HEAD_DIM), F32), jnp.zeros((tq, v_ref.shape[1]), F32))
        st = lax.fori_loop(0, i, sum_step, (init, init))
        v = v_ref[pl.ds(r0, tq), :]
        outs = []
        for c in range(2):
            p = jnp.exp2(jnp.where(allowed, scores(qx[c], c, r0) + ahead, MASK_VALUE))
            denom = jnp.sum(st[c][0] + fold(p, jnp.add), axis=-1, keepdims=True)
            outs.append((st[c][1] + jnp.dot(p.astype(BF16), v, preferred_element_type=F32)) / denom)
        out = outs[0] - lam * outs[1]
        out = _rms(out, sg_ref[...]) * (1.0 - lambda_init)
        o_ref[pl.ds(r0, tq), :] = out.astype(o_ref.dtype)
        return carry

    lax.fori_loop(0, n_blocks, q_block, 0)


def diff_attention(proj, q_col, k_col, v_col, heads, bsz, seq, q_gain, k_gain, lam_vecs, subln_gain,
                   lambda_init):
    slopes = LOG2E * 2.0 ** (-8.0 * jnp.arange(1, heads + 1, dtype=F32) / heads)
    slopes = jnp.broadcast_to(slopes[:, None, None], (heads, 1, HEAD_DIM))
    qk_block = lambda col: pl.BlockSpec((seq, HEAD_DIM), lambda b, h: (b, col + 2 * h))
    vec = lambda n: pl.BlockSpec((1, n), lambda b, h: (0, 0))
    return pl.pallas_call(
        functools.partial(_da_kernel, scale=LOG2E * HEAD_DIM ** -0.5, lambda_init=lambda_init),
        grid=(bsz, heads),
        in_specs=[qk_block(q_col), qk_block(q_col + 1), qk_block(k_col), qk_block(k_col + 1),
                  pl.BlockSpec((seq, 2 * HEAD_DIM), lambda b, h: (b, v_col + h)),
                  vec(HEAD_DIM), vec(HEAD_DIM),
                  pl.BlockSpec((4, HEAD_DIM), lambda b, h: (0, 0)),
                  pl.BlockSpec((1, 1, HEAD_DIM), lambda b, h: (h, 0, 0)),
                  vec(2 * HEAD_DIM)],
        out_specs=pl.BlockSpec((seq, 2 * HEAD_DIM), lambda b, h: (b, h)),
        out_shape=jax.ShapeDtypeStruct((bsz * seq, heads * 2 * HEAD_DIM), BF16),
        scratch_shapes=[pltpu.VMEM((2, seq, 2 * HEAD_DIM), BF16)],
        compiler_params=_params("parallel", "parallel"),
    )(proj, proj, proj, proj, proj, q_gain.reshape(1, HEAD_DIM), k_gain.reshape(1, HEAD_DIM),
      lam_vecs, slopes, subln_gain.reshape(1, 2 * HEAD_DIM))


def _merge_kernel(ya_ref, yb_ref, yc_ref, wa_ref, wb_ref, wc_ref, ga_ref, gb_ref, gc_ref, o_ref):
    def branch(y_ref, w_ref, g_ref):
        out = jnp.dot(y_ref[...], w_ref[...], preferred_element_type=F32)
        return _sigmoid(g_ref[...].astype(F32)) * out

    merged = (branch(ya_ref, wa_ref, ga_ref) + branch(yb_ref, wb_ref, gb_ref)
              + branch(yc_ref, wc_ref, gc_ref))
    o_ref[...] = merged.astype(o_ref.dtype)


def gated_merge(ys, ws, layer, proj, gate_col, *, tm=1024, tn=1024):
    t = ys[0].shape[0]
    d = ws[0].shape[2]
    tm, tn = _tile(t, tm, align=8), _tile(math.gcd(d, gate_col), tn)
    y_spec = lambda y: pl.BlockSpec((tm, y.shape[1]), lambda j, i: (i, 0))
    w_spec = lambda w: pl.BlockSpec((None, w.shape[1], tn), lambda j, i: (layer, 0, j))
    g_spec = lambda b: pl.BlockSpec((tm, tn), lambda j, i: (i, (gate_col + b * d) // tn + j))
    return pl.pallas_call(
        _merge_kernel,
        grid=(d // tn, t // tm),
        in_specs=[y_spec(y) for y in ys] + [w_spec(w) for w in ws] + [g_spec(b) for b in range(3)],
        out_specs=pl.BlockSpec((tm, tn), lambda j, i: (i, j)),
        out_shape=jax.ShapeDtypeStruct((t, d), BF16),
        compiler_params=_params("parallel", "parallel"),
    )(*ys, *ws, proj, proj, proj)


def kernel(x, norm_mix, w_in, ssm_a_re, ssm_a_im, ssm_log_dt, ssm_b_re, ssm_b_im, ssm_c_re, ssm_c_im, ssm_d, ssm_w_glu, ssm_b_glu, ca_q_gain, ca_k_gain, ca_rel_bias, da_q_gain, da_k_gain, da_lam_q1, da_lam_k1, da_lam_q2, da_lam_k2, da_subln_gain, w_out_a, w_out_b, w_out_c, w_o, norm_mlp, w_ff1, w_ff2):
    bsz, seq, d_model = x.shape
    depth = w_in.shape[0]
    ssm_w, ca_w, da_w = w_out_a.shape[1], w_out_b.shape[1], w_out_c.shape[1]
    ca_heads, da_heads = ca_w // HEAD_DIM, da_w // (2 * HEAD_DIM)
    n_chunks = seq // SSM_CHUNK
    ca_q = ssm_w // HEAD_DIM
    ca_k, ca_v = ca_q + ca_heads, ca_q + 2 * ca_heads
    da_q = ca_q + 3 * ca_heads
    da_k, da_v = da_q + 2 * da_heads, da_q + 4 * da_heads
    gate_col = ssm_w + 3 * ca_w + 3 * da_w

    w_glu, w_o, w_ff2 = ssm_w_glu.astype(BF16), w_o.astype(BF16), w_ff2.astype(BF16)
    w_out = tuple(w.astype(BF16) for w in (w_out_a, w_out_b, w_out_c))
    stream_f32 = dict(rows_outer=True, tm=2048, tn=512)

    xt = x.reshape(bsz * seq, d_model)
    for l in range(depth):
        lambda_init = 0.8 - 0.6 * math.exp(-0.3 * l)
        h = rmsnorm(xt, norm_mix[l])
        proj = matmul(h, w_in, l, out_dtype=BF16, **stream_f32)

        mats = _s5_matrices(ssm_a_re[l], ssm_a_im[l], ssm_log_dt[l], ssm_b_re[l], ssm_b_im[l],
                            ssm_c_re[l], ssm_c_im[l], n_chunks)
        y_s5 = s5_scan(proj, mats, bsz, seq, ssm_w)
        y_a = s5_glu(y_s5, proj, ssm_d[l], w_glu, l, ssm_b_glu[l])

        y_b = chunk_band_attention(proj, ca_q, ca_k, ca_v, ca_heads, bsz, seq,
                                   ca_q_gain[l], ca_k_gain[l], ca_rel_bias[l])
        lam_vecs = jnp.stack([da_lam_q1[l], da_lam_k1[l], da_lam_q2[l], da_lam_k2[l]]).astype(F32)
        y_c = diff_attention(proj, da_q, da_k, da_v // 2, da_heads, bsz, seq,
                             da_q_gain[l], da_k_gain[l], lam_vecs, da_subln_gain[l], lambda_init)

        merged = gated_merge((y_a, y_b, y_c), w_out, l, proj, gate_col)
        xt = matmul(merged, w_o, l, out_dtype=F32, epilogue="residual", residual=xt)
        h = rmsnorm(xt, norm_mlp[l])
        ff = matmul(h, w_ff1, l, out_dtype=BF16, epilogue="relu2", **stream_f32)
        xt = matmul(ff, w_ff2, l, out_dtype=F32, epilogue="residual", residual=xt, tk=4096)
    return xt.reshape(bsz, seq, d_model)
```

```python
import functools
import math

import jax
import jax.numpy as jnp
from jax import lax
from jax.experimental import pallas as pl
from jax.experimental.pallas import tpu as pltpu

F32 = jnp.float32
BF16 = jnp.bfloat16

EPS = 1e-6
HEAD_DIM = 128
CHUNK = 64
SSM_GROUP = 16
SSM_CHUNK = 16
S5_LANES = 128
CA_LEFT_CHUNKS = 8
REL_CLIP = 128
MASK_VALUE = -1e30
LOG2E = math.log2(math.e)

V7X_VMEM_BYTES = 64 * 1024 * 1024
VMEM_LIMIT = V7X_VMEM_BYTES - 4 * 1024 * 1024


def _params(*semantics):
    return pltpu.CompilerParams(dimension_semantics=semantics, vmem_limit_bytes=VMEM_LIMIT)


def _rms(x, gain):
    return x * lax.rsqrt(jnp.mean(x * x, axis=-1, keepdims=True) + EPS) * gain


def _sigmoid(x):
    return 1.0 / (1.0 + jnp.exp(-x))


def _tile(dim, preferred, align=128):
    best = None
    for t in range(align, min(dim, preferred) + 1, align):
        if dim % t == 0:
            best = t
    assert best is not None, (dim, preferred)
    return best


def _rmsnorm_kernel(x_ref, g_ref, o_ref):
    o_ref[...] = _rms(x_ref[...], g_ref[...]).astype(o_ref.dtype)


def rmsnorm(x, gain, *, tm=256):
    t, d = x.shape
    tm = _tile(t, tm, align=8)
    return pl.pallas_call(
        _rmsnorm_kernel,
        grid=(t // tm,),
        in_specs=[pl.BlockSpec((tm, d), lambda i: (i, 0)),
                  pl.BlockSpec((1, d), lambda i: (0, 0))],
        out_specs=pl.BlockSpec((tm, d), lambda i: (i, 0)),
        out_shape=jax.ShapeDtypeStruct((t, d), BF16),
        compiler_params=_params("parallel"),
    )(x, gain.reshape(1, d))


def _matmul_kernel(*refs, epilogue, nk):
    if epilogue == "residual":
        a_ref, w_ref, r_ref, o_ref = refs
    else:
        a_ref, w_ref, o_ref = refs
    acc = jnp.dot(a_ref[...], w_ref[...].astype(BF16), preferred_element_type=F32)
    if nk == 1:
        if epilogue == "residual":
            acc = acc + r_ref[...]
        elif epilogue == "relu2":
            acc = jnp.square(jnp.maximum(acc, 0.0))
        o_ref[...] = acc.astype(o_ref.dtype)
    else:
        k = pl.program_id(2)

        @pl.when(k == 0)
        def _():
            o_ref[...] = acc + r_ref[...]

        @pl.when(k > 0)
        def _():
            o_ref[...] += acc


def matmul(a, w, layer, *, out_dtype, epilogue="none", residual=None, tm=1024, tn=1024, tk=None,
           rows_outer=False):
    m, kdim = a.shape
    _, _, n = w.shape
    tm, tn = _tile(m, tm), _tile(n, tn)
    tk = kdim if tk is None else _tile(kdim, tk)
    nk = kdim // tk
    if nk > 1:
        assert epilogue == "residual" and out_dtype == F32
    if rows_outer:
        grid = (m // tm, n // tn, nk)
        ij = lambda g0, g1: (g0, g1)
        a_mode = dict(pipeline_mode=pl.Buffered(1)) if nk == 1 else {}
    else:
        grid = (n // tn, m // tm, nk)
        ij = lambda g0, g1: (g1, g0)
        a_mode = {}
    in_specs = [pl.BlockSpec((tm, tk), lambda g0, g1, k: (ij(g0, g1)[0], k), **a_mode),
                pl.BlockSpec((None, tk, tn), lambda g0, g1, k: (layer, k, ij(g0, g1)[1]))]
    args = [a, w]
    if epilogue == "residual":
        in_specs.append(pl.BlockSpec((tm, tn), lambda g0, g1, k: ij(g0, g1)))
        args.append(residual)
    return pl.pallas_call(
        functools.partial(_matmul_kernel, epilogue=epilogue, nk=nk),
        grid=grid,
        in_specs=in_specs,
        out_specs=pl.BlockSpec((tm, tn), lambda g0, g1, k: ij(g0, g1)),
        out_shape=jax.ShapeDtypeStruct((m, n), out_dtype),
        compiler_params=_params("parallel", "parallel", "arbitrary"),
    )(*args)


def _s5_matrices(a_re, a_im, log_dt, b_re, b_im, c_re, c_im, n_chunks):
    L = SSM_CHUNK
    g, p = a_re.shape
    h = b_re.shape[-1]
    gt = S5_LANES // h
    lt = g // gt

    def cmul(xr, xi, yr, yi):
        return xr * yr - xi * yi, xr * yi + xi * yr

    lam_r, lam_i = jnp.minimum(a_re.astype(F32), -1e-4), a_im.astype(F32)
    dt = jnp.exp(log_dt.astype(F32))[:, None]
    z_r, z_i = lam_r * dt, lam_i * dt

    def a_pow(n):
        mag, ang = jnp.exp(z_r[:, None, :] * n[None, :, None]), z_i[:, None, :] * n[None, :, None]
        return mag * jnp.cos(ang), mag * jnp.sin(ang)

    ab_r, ab_i = jnp.exp(z_r) * jnp.cos(z_i), jnp.exp(z_r) * jnp.sin(z_i)
    den = lam_r * lam_r + lam_i * lam_i
    f_r = ((ab_r - 1.0) * lam_r + ab_i * lam_i) / den
    f_i = (ab_i * lam_r - (ab_r - 1.0) * lam_i) / den
    bb_r, bb_i = cmul(f_r[:, None, :], f_i[:, None, :],
                      b_re.astype(F32).transpose(0, 2, 1), b_im.astype(F32).transpose(0, 2, 1))
    pw_r, pw_i = a_pow(jnp.arange(L + 1, dtype=F32))
    cr, ci = c_re.astype(F32)[:, None], c_im.astype(F32)[:, None]

    ca_r, ca_i = cmul(cr, ci, pw_r[:, :L, None, :], pw_i[:, :L, None, :])
    kern = jnp.einsum("glip,gjp->glji", ca_r, bb_r, precision=lax.Precision.HIGHEST)
    kern = kern - jnp.einsum("glip,gjp->glji", ca_i, bb_i, precision=lax.Precision.HIGHEST)
    ws_r, ws_i = cmul(pw_r[:, L - 1::-1, None, :], pw_i[:, L - 1::-1, None, :],
                      bb_r[:, None], bb_i[:, None])
    wc_r, wc_i = cmul(cr, ci, pw_r[:, 1:, None, :], pw_i[:, 1:, None, :])

    def tile_rows(a):
        n = a.shape[-1]
        return a.reshape(lt, gt, L, h, n).transpose(0, 2, 1, 3, 4).reshape(lt, L * gt * h, n)

    kern = tile_rows(kern)
    wsum = tile_rows(jnp.concatenate([ws_r, ws_i], axis=-1))
    wcar = tile_rows(jnp.concatenate([wc_r, -wc_i], axis=-1))
    n_steps = int(math.log2(n_chunks))
    ak_r, ak_i = a_pow(L * 2.0 ** jnp.arange(n_steps, dtype=F32))
    tile_cols = lambda a: a.reshape(lt, gt, n_steps, 2 * p).transpose(0, 2, 1, 3).reshape(
        lt, n_steps, gt * 2 * p)
    ar = tile_cols(jnp.concatenate([ak_r, ak_r], axis=-1))
    ai = tile_cols(jnp.concatenate([-ak_i, ak_i], axis=-1))
    return kern.astype(BF16), wsum.astype(BF16), wcar.astype(BF16), ar, ai


def _s5_kernel(u_ref, kern_ref, wsum_ref, wcar_ref, ar_ref, ai_ref, y_ref,
               x_ref, dcat_ref, wsum_full_ref, wcar_full_ref, *, row_tile):
    L = SSM_CHUNK
    seq, lanes = u_ref.shape
    n_chunks = seq // L
    h = kern_ref.shape[-1]
    gt = lanes // h
    state_w = wsum_ref.shape[-1]

    @pl.when(pl.program_id(1) == 0)
    def _():
        rows = kern_ref.shape[1]
        row_group = (lax.broadcasted_iota(jnp.int32, (rows, lanes), 0) // h) % gt
        lane = lax.broadcasted_iota(jnp.int32, (rows, lanes), 1)
        spread = (lax.broadcasted_iota(jnp.int32, (h, lanes), 1) % h
                  == lax.broadcasted_iota(jnp.int32, (h, lanes), 0)).astype(BF16)
        tiled = jnp.dot(kern_ref[0], spread, preferred_element_type=F32)
        dcat_ref[...] = jnp.where(row_group == lane // h, tiled, 0.0).astype(BF16)
        for grp in range(gt):
            cols = slice(grp * state_w, (grp + 1) * state_w)
            wsum_full_ref[:, cols] = jnp.where(row_group == grp, wsum_ref[0], 0.0)
            wcar_full_ref[:, cols] = jnp.where(row_group == grp, wcar_ref[0], 0.0)

    step = lax.broadcasted_iota(jnp.int32, (row_tile, lanes), 0) % L

    def intra(i, carry):
        r0 = pl.multiple_of(i * row_tile, row_tile)
        x = u_ref[pl.ds(r0, row_tile), :].astype(F32)
        x_ref[pl.ds(r0, row_tile), :] = x
        lagged = [x.astype(BF16)] + [
            jnp.where(step >= lag, pltpu.roll(x, lag, 0), 0.0).astype(BF16) for lag in range(1, L)]
        y_ref[pl.ds(r0, row_tile), :] = jnp.dot(jnp.concatenate(lagged, axis=1), dcat_ref[...],
                                                preferred_element_type=F32)
        return carry

    lax.fori_loop(0, seq // row_tile, intra, 0)

    steps = [x_ref[pl.ds(s, n_chunks, stride=L), :].astype(BF16) for s in range(L)]
    state = jnp.dot(jnp.concatenate(steps, axis=1), wsum_full_ref[...],
                    preferred_element_type=F32)
    chunk = lax.broadcasted_iota(jnp.int32, state.shape, 0)

    def swap_re_im(a):
        return jnp.concatenate(
            [pltpu.roll(a[:, grp * state_w:(grp + 1) * state_w], state_w // 2, 1)
             for grp in range(gt)], axis=1)

    for k in range(ar_ref.shape[1]):
        shift = 1 << k
        prev = jnp.where(chunk >= shift, pltpu.roll(state, shift, 0), 0.0)
        state = state + ar_ref[0, k:k + 1, :] * prev + ai_ref[0, k:k + 1, :] * swap_re_im(prev)
    incoming = jnp.where(chunk >= 1, pltpu.roll(state, 1, 0), 0.0)
    carried = lax.dot_general(incoming.astype(BF16), wcar_full_ref[...], (((1,), (1,)), ((), ())),
                              preferred_element_type=F32)
    for t in range(L):
        rows = pl.ds(t, n_chunks, stride=L)
        y_ref[rows, :] = y_ref[rows, :] + carried[:, t * lanes:(t + 1) * lanes]


def s5_scan(proj, mats, bsz, seq, width, *, row_tile=512):
    kern, wsum, wcar, ar, ai = mats
    rows, state_w = wsum.shape[1:]
    gt = S5_LANES // kern.shape[-1]
    assert state_w == S5_LANES
    blk = lambda a: pl.BlockSpec((1,) + a.shape[1:], lambda j, b: (j, 0, 0))
    return pl.pallas_call(
        functools.partial(_s5_kernel, row_tile=_tile(seq, row_tile)),
        grid=(width // S5_LANES, bsz),
        in_specs=[pl.BlockSpec((seq, S5_LANES), lambda j, b: (b, j)),
                  blk(kern), blk(wsum), blk(wcar), blk(ar), blk(ai)],
        out_specs=pl.BlockSpec((seq, S5_LANES), lambda j, b: (b, j)),
        out_shape=jax.ShapeDtypeStruct((bsz * seq, width), F32),
        scratch_shapes=[pltpu.VMEM((seq, S5_LANES), F32),
                        pltpu.VMEM((rows, S5_LANES), BF16),
                        pltpu.VMEM((rows, gt * state_w), BF16),
                        pltpu.VMEM((rows, gt * state_w), BF16)],
        compiler_params=_params("arbitrary", "arbitrary"),
    )(proj, kern, wsum, wcar, ar, ai)


def _s5_glu_kernel(y_ref, u_ref, d_ref, w_ref, b_ref, o_ref):
    y = y_ref[...] + d_ref[...] * u_ref[...].astype(F32)
    z = y * (0.5 * (1.0 + jnp.tanh(math.sqrt(2.0 / math.pi) * (y + 0.044715 * (y * y * y)))))
    gate = jnp.dot(z.astype(BF16), w_ref[...], preferred_element_type=F32) + b_ref[...]
    o_ref[...] = (z * _sigmoid(gate)).astype(o_ref.dtype)


def s5_glu(y, proj, d_skip, w_glu, layer, b_glu, *, tm=512):
    t, w = y.shape
    tm = _tile(t, tm, align=8)
    row = lambda a: pl.BlockSpec((1, w), lambda i: (0, 0))
    return pl.pallas_call(
        _s5_glu_kernel,
        grid=(t // tm,),
        in_specs=[pl.BlockSpec((tm, w), lambda i: (i, 0)),
                  pl.BlockSpec((tm, w), lambda i: (i, 0)),
                  row(d_skip),
                  pl.BlockSpec((None, w, w), lambda i: (layer, 0, 0)),
                  row(b_glu)],
        out_specs=pl.BlockSpec((tm, w), lambda i: (i, 0)),
        out_shape=jax.ShapeDtypeStruct((t, w), BF16),
        compiler_params=_params("parallel"),
    )(y, proj, d_skip.reshape(1, w), w_glu, b_glu.reshape(1, w))


CA_QBLOCK = 4 * CHUNK
CA_WINDOW = CA_LEFT_CHUNKS * CHUNK + CA_QBLOCK
CA_LEAD = CA_LEFT_CHUNKS * CHUNK // CA_QBLOCK


def _ca_bias(rel_bias):
    heads = rel_bias.shape[0]
    o = jnp.arange(CA_LEAD + 1)[:, None, None]
    q_pos = o * CA_QBLOCK + jnp.arange(CA_QBLOCK)[None, :, None]
    k_pos = jnp.arange(CA_WINDOW)[None, None, :]
    q_chunk, k_chunk = q_pos // CHUNK, k_pos // CHUNK
    allowed = (k_chunk <= q_chunk) & (k_chunk >= q_chunk - CA_LEFT_CHUNKS)
    period = CA_WINDOW + CA_QBLOCK
    n = jnp.arange(period)
    k_minus_q = jnp.where(n < CA_WINDOW, n, n - period)
    idx = jnp.clip(o[:, :, 0] * CA_QBLOCK - k_minus_q[None, :], -REL_CLIP, REL_CLIP) + REL_CLIP
    line = LOG2E * rel_bias.astype(F32)[:, idx]
    tiled = jnp.tile(line, (1, 1, CA_QBLOCK))[..., :CA_QBLOCK * (period - 1)]
    bias = tiled.reshape(heads, CA_LEAD + 1, CA_QBLOCK, period - 1)[..., :CA_WINDOW]
    return jnp.where(allowed[None], bias, MASK_VALUE).transpose(1, 0, 2, 3)


def _ca_kernel(q_ref, k_ref, v_ref, bias_ref, qg_ref, kg_ref, o_ref, kn_ref, *, scale):
    kn_ref[...] = _rms(k_ref[...].astype(F32), kg_ref[...]).astype(BF16)
    n_blocks = q_ref.shape[0] // CA_QBLOCK

    def body(i, carry):
        r0 = pl.multiple_of(i * CA_QBLOCK, CA_QBLOCK)
        w0 = pl.multiple_of(jnp.maximum(i - CA_LEAD, 0) * CA_QBLOCK, CA_QBLOCK)
        q = (_rms(q_ref[pl.ds(r0, CA_QBLOCK), :].astype(F32), qg_ref[...]) * scale).astype(BF16)
        s = lax.dot_general(q, kn_ref[pl.ds(w0, CA_WINDOW), :], (((1,), (1,)), ((), ())),
                            preferred_element_type=F32)
        s = s + bias_ref[jnp.minimum(i, CA_LEAD), 0]
        p = jnp.exp2(s - jnp.max(s, axis=-1, keepdims=True))
        denom = jnp.sum(p, axis=-1, keepdims=True)
        out = jnp.dot(p.astype(BF16), v_ref[pl.ds(w0, CA_WINDOW), :], preferred_element_type=F32)
        o_ref[pl.ds(r0, CA_QBLOCK), :] = (out / denom).astype(o_ref.dtype)
        return carry

    lax.fori_loop(0, n_blocks, body, 0, unroll=4)


def chunk_band_attention(proj, q_col, k_col, v_col, heads, bsz, seq, q_gain, k_gain, rel_bias):
    bias = _ca_bias(rel_bias)
    head_block = lambda col: pl.BlockSpec((seq, HEAD_DIM), lambda h, b: (b, col + h))
    gain = pl.BlockSpec((1, HEAD_DIM), lambda h, b: (0, 0))
    return pl.pallas_call(
        functools.partial(_ca_kernel, scale=LOG2E * HEAD_DIM ** -0.5),
        grid=(heads, bsz),
        in_specs=[head_block(q_col), head_block(k_col), head_block(v_col),
                  pl.BlockSpec((CA_LEAD + 1, 1, CA_QBLOCK, CA_WINDOW), lambda h, b: (0, h, 0, 0)),
                  gain, gain],
        out_specs=pl.BlockSpec((seq, HEAD_DIM), lambda h, b: (b, h)),
        out_shape=jax.ShapeDtypeStruct((bsz * seq, heads * HEAD_DIM), BF16),
        scratch_shapes=[pltpu.VMEM((seq, HEAD_DIM), BF16)],
        compiler_params=_params("parallel", "parallel"),
    )(proj, proj, proj, bias, q_gain.reshape(1, HEAD_DIM), k_gain.reshape(1, HEAD_DIM))


DA_BLOCK = 512


def _split3(x):
    hi = x.astype(BF16)
    rest = x - hi.astype(F32)
    mid = rest.astype(BF16)
    return hi, mid, (rest - mid.astype(F32)).astype(BF16)


def _da_kernel(q1_ref, q2_ref, k1_ref, k2_ref, v_ref, qg_ref, kg_ref, lam_ref, slope_ref, sg_ref,
               o_ref, kx_ref, *, scale, lambda_init):
    tq = DA_BLOCK
    seq = q1_ref.shape[0]
    n_blocks = seq // tq
    slope = slope_ref[0, :, 0:1]
    lam = (jnp.exp(jnp.sum(lam_ref[0:1, :] * lam_ref[1:2, :], axis=-1, keepdims=True))
           - jnp.exp(jnp.sum(lam_ref[2:3, :] * lam_ref[3:4, :], axis=-1, keepdims=True))
           + lambda_init)

    def extra_columns(shape, first, second):
        lane = lax.broadcasted_iota(jnp.int32, shape, 1)
        out = jnp.zeros(shape, F32)
        for offset, pieces in ((0, first), (3, second)):
            for n, piece in enumerate(pieces):
                piece = piece if isinstance(piece, float) else piece.astype(F32)
                out = jnp.where(lane == offset + n, piece, out)
        return out.astype(BF16)

    ones = (1.0, 1.0, 1.0)
    key_pos = lax.broadcasted_iota(jnp.int32, (seq, 1), 0).astype(F32)
    key_extra = extra_columns((seq, HEAD_DIM), _split3(slope * key_pos), ones)
    for c, k_ref in enumerate((k1_ref, k2_ref)):
        kx_ref[c, :, 0:HEAD_DIM] = _rms(k_ref[...].astype(F32), kg_ref[...]).astype(BF16)
        kx_ref[c, :, HEAD_DIM:] = key_extra

    row = lax.broadcasted_iota(jnp.int32, (tq, tq), 0)
    col = lax.broadcasted_iota(jnp.int32, (tq, tq), 1)
    ahead = jnp.where(col > row, (2.0 * slope) * (row - col).astype(F32), 0.0)
    allowed = (col // CHUNK) <= (row // CHUNK)

    def scores(qx, c, c0):
        return lax.dot_general(qx, kx_ref[c, pl.ds(c0, tq), :], (((1,), (1,)), ((), ())),
                               preferred_element_type=F32)

    def fold(a, op):
        parts = [a[:, n * HEAD_DIM:(n + 1) * HEAD_DIM] for n in range(tq // HEAD_DIM)]
        while len(parts) > 1:
            parts = [op(parts[n], parts[n + 1]) for n in range(0, len(parts), 2)]
        return parts[0]

    def past_blocks(n, step, state):
        pairs = n // 2
        state = lax.fori_loop(0, pairs, lambda t, st: step(2 * t + 1, step(2 * t, st)), state)
        return lax.fori_loop(2 * pairs, n, step, state)

    def q_block(i, carry):
        r0 = pl.multiple_of(i * tq, tq)
        qn = [(_rms(q_ref[pl.ds(r0, tq), :].astype(F32), qg_ref[...]) * scale).astype(BF16)
              for q_ref in (q1_ref, q2_ref)]
        zeros3 = (0.0, 0.0, 0.0)
        qx = [jnp.concatenate([q, extra_columns((tq, HEAD_DIM), ones, zeros3)], axis=1) for q in qn]

        def max_step(j, mx):
            c0 = pl.multiple_of(j * tq, tq)
            return tuple(jnp.maximum(mx[c], fold(scores(qx[c], c, c0), jnp.maximum))
                         for c in range(2))

        start = jnp.full((tq, HEAD_DIM), MASK_VALUE, F32)
        mx = past_blocks(i, max_step, (start, start))
        row_max = [jnp.maximum(
            jnp.max(mx[c], axis=-1, keepdims=True),
            jnp.max(jnp.where(allowed, scores(qx[c], c, r0) + ahead, MASK_VALUE),
                    axis=-1, keepdims=True)) for c in range(2)]
        qx = [jnp.concatenate([qn[c], extra_columns((tq, HEAD_DIM), ones, _split3(-row_max[c]))],
                              axis=1) for c in range(2)]

        def sum_step(j, st):
            c0 = pl.multiple_of(j * tq, tq)
            v = v_ref[pl.ds(c0, tq), :]
            out = []
            for c in range(2):
                p = jnp.exp2(scores(qx[c], c, c0))
                out.append((st[c][0] + fold(p, jnp.add),
                            st[c][1] + jnp.dot(p.astype(BF16), v, preferred_element_type=F32)))
            return tuple(out)

        init = (jnp.zeros((tq, HEAD_DIM), F32), jnp.zeros((tq, v_ref.shape[1]), F32))
        st = past_blocks(i, sum_step, (init, init))
        v = v_ref[pl.ds(r0, tq), :]
        outs = []
        for c in range(2):
            p = jnp.exp2(jnp.where(allowed, scores(qx[c], c, r0) + ahead, MASK_VALUE))
            denom = jnp.sum(st[c][0] + fold(p, jnp.add), axis=-1, keepdims=True)
            outs.append((st[c][1] + jnp.dot(p.astype(BF16), v, preferred_element_type=F32)) / denom)
        out = outs[0] - lam * outs[1]
        out = _rms(out, sg_ref[...]) * (1.0 - lambda_init)
        o_ref[pl.ds(r0, tq), :] = out.astype(o_ref.dtype)
        return carry

    lax.fori_loop(0, n_blocks, q_block, 0)


def diff_attention(proj, q_col, k_col, v_col, heads, bsz, seq, q_gain, k_gain, lam_vecs, subln_gain,
                   lambda_init):
    slopes = LOG2E * 2.0 ** (-8.0 * jnp.arange(1, heads + 1, dtype=F32) / heads)
    slopes = jnp.broadcast_to(slopes[:, None, None], (heads, 1, HEAD_DIM))
    qk_block = lambda col: pl.BlockSpec((seq, HEAD_DIM), lambda b, h: (b, col + 2 * h))
    vec = lambda n: pl.BlockSpec((1, n), lambda b, h: (0, 0))
    return pl.pallas_call(
        functools.partial(_da_kernel, scale=LOG2E * HEAD_DIM ** -0.5, lambda_init=lambda_init),
        grid=(bsz, heads),
        in_specs=[qk_block(q_col), qk_block(q_col + 1), qk_block(k_col), qk_block(k_col + 1),
                  pl.BlockSpec((seq, 2 * HEAD_DIM), lambda b, h: (b, v_col + h)),
                  vec(HEAD_DIM), vec(HEAD_DIM),
                  pl.BlockSpec((4, HEAD_DIM), lambda b, h: (0, 0)),
                  pl.BlockSpec((1, 1, HEAD_DIM), lambda b, h: (h, 0, 0)),
                  vec(2 * HEAD_DIM)],
        out_specs=pl.BlockSpec((seq, 2 * HEAD_DIM), lambda b, h: (b, h)),
        out_shape=jax.ShapeDtypeStruct((bsz * seq, heads * 2 * HEAD_DIM), BF16),
        scratch_shapes=[pltpu.VMEM((2, seq, 2 * HEAD_DIM), BF16)],
        compiler_params=_params("parallel", "parallel"),
    )(proj, proj, proj, proj, proj, q_gain.reshape(1, HEAD_DIM), k_gain.reshape(1, HEAD_DIM),
      lam_vecs, slopes, subln_gain.reshape(1, 2 * HEAD_DIM))


def _merge_kernel(ya_ref, yb_ref, yc_ref, wa_ref, wb_ref, wc_ref, ga_ref, gb_ref, gc_ref, o_ref):
    def branch(y_ref, w_ref, g_ref):
        out = jnp.dot(y_ref[...], w_ref[...], preferred_element_type=F32)
        return _sigmoid(g_ref[...].astype(F32)) * out

    merged = (branch(ya_ref, wa_ref, ga_ref) + branch(yb_ref, wb_ref, gb_ref)
              + branch(yc_ref, wc_ref, gc_ref))
    o_ref[...] = merged.astype(o_ref.dtype)


def gated_merge(ys, ws, layer, proj, gate_col, *, tm=1024, tn=1024):
    t = ys[0].shape[0]
    d = ws[0].shape[2]
    tm, tn = _tile(t, tm, align=8), _tile(math.gcd(d, gate_col), tn)
    y_spec = lambda y: pl.BlockSpec((tm, y.shape[1]), lambda j, i: (i, 0))
    w_spec = lambda w: pl.BlockSpec((None, w.shape[1], tn), lambda j, i: (layer, 0, j))
    g_spec = lambda b: pl.BlockSpec((tm, tn), lambda j, i: (i, (gate_col + b * d) // tn + j))
    return pl.pallas_call(
        _merge_kernel,
        grid=(d // tn, t // tm),
        in_specs=[y_spec(y) for y in ys] + [w_spec(w) for w in ws] + [g_spec(b) for b in range(3)],
        out_specs=pl.BlockSpec((tm, tn), lambda j, i: (i, j)),
        out_shape=jax.ShapeDtypeStruct((t, d), BF16),
        compiler_params=_params("parallel", "parallel"),
    )(*ys, *ws, proj, proj, proj)


def kernel(x, norm_mix, w_in, ssm_a_re, ssm_a_im, ssm_log_dt, ssm_b_re, ssm_b_im, ssm_c_re, ssm_c_im, ssm_d, ssm_w_glu, ssm_b_glu, ca_q_gain, ca_k_gain, ca_rel_bias, da_q_gain, da_k_gain, da_lam_q1, da_lam_k1, da_lam_q2, da_lam_k2, da_subln_gain, w_out_a, w_out_b, w_out_c, w_o, norm_mlp, w_ff1, w_ff2):
    bsz, seq, d_model = x.shape
    depth = w_in.shape[0]
    ssm_w, ca_w, da_w = w_out_a.shape[1], w_out_b.shape[1], w_out_c.shape[1]
    ca_heads, da_heads = ca_w // HEAD_DIM, da_w // (2 * HEAD_DIM)
    n_chunks = seq // SSM_CHUNK
    ca_q = ssm_w // HEAD_DIM
    ca_k, ca_v = ca_q + ca_heads, ca_q + 2 * ca_heads
    da_q = ca_q + 3 * ca_heads
    da_k, da_v = da_q + 2 * da_heads, da_q + 4 * da_heads
    gate_col = ssm_w + 3 * ca_w + 3 * da_w

    w_glu, w_o, w_ff2 = ssm_w_glu.astype(BF16), w_o.astype(BF16), w_ff2.astype(BF16)
    w_out = tuple(w.astype(BF16) for w in (w_out_a, w_out_b, w_out_c))
    stream_f32 = dict(rows_outer=True, tm=2048, tn=512)

    xt = x.reshape(bsz * seq, d_model)
    for l in range(depth):
        lambda_init = 0.8 - 0.6 * math.exp(-0.3 * l)
        h = rmsnorm(xt, norm_mix[l])
        proj = matmul(h, w_in, l, out_dtype=BF16, **stream_f32)

        mats = _s5_matrices(ssm_a_re[l], ssm_a_im[l], ssm_log_dt[l], ssm_b_re[l], ssm_b_im[l],
                            ssm_c_re[l], ssm_c_im[l], n_chunks)
        y_s5 = s5_scan(proj, mats, bsz, seq, ssm_w)
        y_a = s5_glu(y_s5, proj, ssm_d[l], w_glu, l, ssm_b_glu[l])

        y_b = chunk_band_attention(proj, ca_q, ca_k, ca_v, ca_heads, bsz, seq,
                                   ca_q_gain[l], ca_k_gain[l], ca_rel_bias[l])
        lam_vecs = jnp.stack([da_lam_q1[l], da_lam_k1[l], da_lam_q2[l], da_lam_k2[l]]).astype(F32)
        y_c = diff_attention(proj, da_q, da_k, da_v // 2, da_heads, bsz, seq,
                             da_q_gain[l], da_k_gain[l], lam_vecs, da_subln_gain[l], lambda_init)

        merged = gated_merge((y_a, y_b, y_c), w_out, l, proj, gate_col)
        xt = matmul(merged, w_o, l, out_dtype=F32, epilogue="residual", residual=xt)
        h = rmsnorm(xt, norm_mlp[l])
        ff = matmul(h, w_ff1, l, out_dtype=BF16, epilogue="relu2", **stream_f32)
        xt = matmul(ff, w_ff2, l, out_dtype=F32, epilogue="residual", residual=xt, tk=4096)
    return xt.reshape(bsz, seq, d_model)
```

```python
import functools
import math

import jax
import jax.numpy as jnp
from jax import lax
from jax.experimental import pallas as pl
from jax.experimental.pallas import tpu as pltpu

F32 = jnp.float32
BF16 = jnp.bfloat16

EPS = 1e-6
HEAD_DIM = 128
CHUNK = 64
SSM_GROUP = 16
SSM_CHUNK = 16
S5_LANES = 128
CA_LEFT_CHUNKS = 8
REL_CLIP = 128
MASK_VALUE = -1e30
LOG2E = math.log2(math.e)

V7X_VMEM_BYTES = 64 * 1024 * 1024
VMEM_LIMIT = V7X_VMEM_BYTES - 4 * 1024 * 1024


def _params(*semantics):
    return pltpu.CompilerParams(dimension_semantics=semantics, vmem_limit_bytes=VMEM_LIMIT)


def _rms(x, gain):
    return x * lax.rsqrt(jnp.mean(x * x, axis=-1, keepdims=True) + EPS) * gain


def _sigmoid(x):
    return 1.0 / (1.0 + jnp.exp(-x))


def _tile(dim, preferred, align=128):
    best = None
    for t in range(align, min(dim, preferred) + 1, align):
        if dim % t == 0:
            best = t
    assert best is not None, (dim, preferred)
    return best


def _rmsnorm_kernel(x_ref, g_ref, o_ref):
    o_ref[...] = _rms(x_ref[...], g_ref[...]).astype(o_ref.dtype)


def rmsnorm(x, gain, *, tm=512):
    t, d = x.shape
    tm = _tile(t, tm, align=8)
    return pl.pallas_call(
        _rmsnorm_kernel,
        grid=(t // tm,),
        in_specs=[pl.BlockSpec((tm, d), lambda i: (i, 0)),
                  pl.BlockSpec((1, d), lambda i: (0, 0))],
        out_specs=pl.BlockSpec((tm, d), lambda i: (i, 0)),
        out_shape=jax.ShapeDtypeStruct((t, d), BF16),
        compiler_params=_params("parallel"),
    )(x, gain.reshape(1, d))


def _matmul_kernel(*refs, epilogue, nk):
    if epilogue == "residual":
        a_ref, w_ref, r_ref, o_ref = refs
    else:
        a_ref, w_ref, o_ref = refs
    acc = jnp.dot(a_ref[...], w_ref[...].astype(BF16), preferred_element_type=F32)
    if nk == 1:
        if epilogue == "residual":
            acc = acc + r_ref[...]
        elif epilogue == "relu2":
            acc = jnp.square(jnp.maximum(acc, 0.0))
        o_ref[...] = acc.astype(o_ref.dtype)
    else:
        k = pl.program_id(2)

        @pl.when(k == 0)
        def _():
            o_ref[...] = acc + r_ref[...]

        @pl.when(k > 0)
        def _():
            o_ref[...] += acc


def matmul(a, w, layer, *, out_dtype, epilogue="none", residual=None, tm=1024, tn=1024, tk=None,
           rows_outer=False):
    m, kdim = a.shape
    _, _, n = w.shape
    tm, tn = _tile(m, tm), _tile(n, tn)
    tk = kdim if tk is None else _tile(kdim, tk)
    nk = kdim // tk
    if nk > 1:
        assert epilogue == "residual" and out_dtype == F32
    if rows_outer:
        grid = (m // tm, n // tn, nk)
        ij = lambda g0, g1: (g0, g1)
        a_mode = dict(pipeline_mode=pl.Buffered(1)) if nk == 1 else {}
    else:
        grid = (n // tn, m // tm, nk)
        ij = lambda g0, g1: (g1, g0)
        a_mode = {}
    in_specs = [pl.BlockSpec((tm, tk), lambda g0, g1, k: (ij(g0, g1)[0], k), **a_mode),
                pl.BlockSpec((None, tk, tn), lambda g0, g1, k: (layer, k, ij(g0, g1)[1]))]
    args = [a, w]
    if epilogue == "residual":
        in_specs.append(pl.BlockSpec((tm, tn), lambda g0, g1, k: ij(g0, g1)))
        args.append(residual)
    return pl.pallas_call(
        functools.partial(_matmul_kernel, epilogue=epilogue, nk=nk),
        grid=grid,
        in_specs=in_specs,
        out_specs=pl.BlockSpec((tm, tn), lambda g0, g1, k: ij(g0, g1)),
        out_shape=jax.ShapeDtypeStruct((m, n), out_dtype),
        compiler_params=_params("parallel", "parallel", "arbitrary"),
    )(*args)


def _s5_matrices(a_re, a_im, log_dt, b_re, b_im, c_re, c_im, n_chunks):
    L = SSM_CHUNK
    g, p = a_re.shape
    h = b_re.shape[-1]
    gt = S5_LANES // h
    lt = g // gt

    def cmul(xr, xi, yr, yi):
        return xr * yr - xi * yi, xr * yi + xi * yr

    lam_r, lam_i = jnp.minimum(a_re.astype(F32), -1e-4), a_im.astype(F32)
    dt = jnp.exp(log_dt.astype(F32))[:, None]
    z_r, z_i = lam_r * dt, lam_i * dt

    def a_pow(n):
        mag, ang = jnp.exp(z_r[:, None, :] * n[None, :, None]), z_i[:, None, :] * n[None, :, None]
        return mag * jnp.cos(ang), mag * jnp.sin(ang)

    ab_r, ab_i = jnp.exp(z_r) * jnp.cos(z_i), jnp.exp(z_r) * jnp.sin(z_i)
    den = lam_r * lam_r + lam_i * lam_i
    f_r = ((ab_r - 1.0) * lam_r + ab_i * lam_i) / den
    f_i = (ab_i * lam_r - (ab_r - 1.0) * lam_i) / den
    bb_r, bb_i = cmul(f_r[:, None, :], f_i[:, None, :],
                      b_re.astype(F32).transpose(0, 2, 1), b_im.astype(F32).transpose(0, 2, 1))
    pw_r, pw_i = a_pow(jnp.arange(L + 1, dtype=F32))
    cr, ci = c_re.astype(F32)[:, None], c_im.astype(F32)[:, None]

    ca_r, ca_i = cmul(cr, ci, pw_r[:, :L, None, :], pw_i[:, :L, None, :])
    kern = jnp.einsum("glip,gjp->glji", ca_r, bb_r, precision=lax.Precision.HIGHEST)
    kern = kern - jnp.einsum("glip,gjp->glji", ca_i, bb_i, precision=lax.Precision.HIGHEST)
    ws_r, ws_i = cmul(pw_r[:, L - 1::-1, None, :], pw_i[:, L - 1::-1, None, :],
                      bb_r[:, None], bb_i[:, None])
    wc_r, wc_i = cmul(cr, ci, pw_r[:, 1:, None, :], pw_i[:, 1:, None, :])

    def tile_rows(a):
        n = a.shape[-1]
        return a.reshape(lt, gt, L, h, n).transpose(0, 2, 1, 3, 4).reshape(lt, L * gt * h, n)

    kern = tile_rows(kern)
    wsum = tile_rows(jnp.concatenate([ws_r, ws_i], axis=-1))
    wcar = tile_rows(jnp.concatenate([wc_r, -wc_i], axis=-1))
    n_steps = int(math.log2(n_chunks))
    ak_r, ak_i = a_pow(L * 2.0 ** jnp.arange(n_steps, dtype=F32))
    tile_cols = lambda a: a.reshape(lt, gt, n_steps, 2 * p).transpose(0, 2, 1, 3).reshape(
        lt, n_steps, gt * 2 * p)
    ar = tile_cols(jnp.concatenate([ak_r, ak_r], axis=-1))
    ai = tile_cols(jnp.concatenate([-ak_i, ak_i], axis=-1))
    return kern.astype(BF16), wsum.astype(BF16), wcar.astype(BF16), ar, ai


def _s5_kernel(u_ref, kern_ref, wsum_ref, wcar_ref, ar_ref, ai_ref, y_ref,
               x_ref, dcat_ref, wsum_full_ref, wcar_full_ref, *, row_tile):
    L = SSM_CHUNK
    seq, lanes = u_ref.shape
    n_chunks = seq // L
    h = kern_ref.shape[-1]
    gt = lanes // h
    state_w = wsum_ref.shape[-1]

    @pl.when(pl.program_id(1) == 0)
    def _():
        rows = kern_ref.shape[1]
        row_group = (lax.broadcasted_iota(jnp.int32, (rows, lanes), 0) // h) % gt
        lane = lax.broadcasted_iota(jnp.int32, (rows, lanes), 1)
        spread = (lax.broadcasted_iota(jnp.int32, (h, lanes), 1) % h
                  == lax.broadcasted_iota(jnp.int32, (h, lanes), 0)).astype(BF16)
        tiled = jnp.dot(kern_ref[0], spread, preferred_element_type=F32)
        dcat_ref[...] = jnp.where(row_group == lane // h, tiled, 0.0).astype(BF16)
        for grp in range(gt):
            cols = slice(grp * state_w, (grp + 1) * state_w)
            wsum_full_ref[:, cols] = jnp.where(row_group == grp, wsum_ref[0], 0.0)
            wcar_full_ref[:, cols] = jnp.where(row_group == grp, wcar_ref[0], 0.0)

    step = lax.broadcasted_iota(jnp.int32, (row_tile, lanes), 0) % L

    def intra(i, carry):
        r0 = pl.multiple_of(i * row_tile, row_tile)
        x = u_ref[pl.ds(r0, row_tile), :].astype(F32)
        x_ref[pl.ds(r0, row_tile), :] = x
        lagged = [x.astype(BF16)] + [
            jnp.where(step >= lag, pltpu.roll(x, lag, 0), 0.0).astype(BF16) for lag in range(1, L)]
        y_ref[pl.ds(r0, row_tile), :] = jnp.dot(jnp.concatenate(lagged, axis=1), dcat_ref[...],
                                                preferred_element_type=F32)
        return carry

    lax.fori_loop(0, seq // row_tile, intra, 0)

    steps = [x_ref[pl.ds(s, n_chunks, stride=L), :].astype(BF16) for s in range(L)]
    state = jnp.dot(jnp.concatenate(steps, axis=1), wsum_full_ref[...],
                    preferred_element_type=F32)
    chunk = lax.broadcasted_iota(jnp.int32, state.shape, 0)

    def swap_re_im(a):
        return jnp.concatenate(
            [pltpu.roll(a[:, grp * state_w:(grp + 1) * state_w], state_w // 2, 1)
             for grp in range(gt)], axis=1)

    for k in range(ar_ref.shape[1]):
        shift = 1 << k
        prev = jnp.where(chunk >= shift, pltpu.roll(state, shift, 0), 0.0)
        state = state + ar_ref[0, k:k + 1, :] * prev + ai_ref[0, k:k + 1, :] * swap_re_im(prev)
    incoming = jnp.where(chunk >= 1, pltpu.roll(state, 1, 0), 0.0)
    carried = lax.dot_general(incoming.astype(BF16), wcar_full_ref[...], (((1,), (1,)), ((), ())),
                              preferred_element_type=F32)
    for t in range(L):
        rows = pl.ds(t, n_chunks, stride=L)
        y_ref[rows, :] = y_ref[rows, :] + carried[:, t * lanes:(t + 1) * lanes]


def s5_scan(proj, mats, bsz, seq, width, *, row_tile=512):
    kern, wsum, wcar, ar, ai = mats
    rows, state_w = wsum.shape[1:]
    gt = S5_LANES // kern.shape[-1]
    assert state_w == S5_LANES
    blk = lambda a: pl.BlockSpec((1,) + a.shape[1:], lambda j, b: (j, 0, 0))
    return pl.pallas_call(
        functools.partial(_s5_kernel, row_tile=_tile(seq, row_tile)),
        grid=(width // S5_LANES, bsz),
        in_specs=[pl.BlockSpec((seq, S5_LANES), lambda j, b: (b, j)),
                  blk(kern), blk(wsum), blk(wcar), blk(ar), blk(ai)],
        out_specs=pl.BlockSpec((seq, S5_LANES), lambda j, b: (b, j)),
        out_shape=jax.ShapeDtypeStruct((bsz * seq, width), F32),
        scratch_shapes=[pltpu.VMEM((seq, S5_LANES), F32),
                        pltpu.VMEM((rows, S5_LANES), BF16),
                        pltpu.VMEM((rows, gt * state_w), BF16),
                        pltpu.VMEM((rows, gt * state_w), BF16)],
        compiler_params=_params("arbitrary", "arbitrary"),
    )(proj, kern, wsum, wcar, ar, ai)


def _s5_glu_kernel(y_ref, u_ref, d_ref, w_ref, b_ref, o_ref):
    y = y_ref[...] + d_ref[...] * u_ref[...].astype(F32)
    z = y * (0.5 * (1.0 + jnp.tanh(math.sqrt(2.0 / math.pi) * (y + 0.044715 * (y * y * y)))))
    gate = jnp.dot(z.astype(BF16), w_ref[...], preferred_element_type=F32) + b_ref[...]
    o_ref[...] = (z * _sigmoid(gate)).astype(o_ref.dtype)


def s5_glu(y, proj, d_skip, w_glu, layer, b_glu, *, tm=512):
    t, w = y.shape
    tm = _tile(t, tm, align=8)
    row = lambda a: pl.BlockSpec((1, w), lambda i: (0, 0))
    return pl.pallas_call(
        _s5_glu_kernel,
        grid=(t // tm,),
        in_specs=[pl.BlockSpec((tm, w), lambda i: (i, 0)),
                  pl.BlockSpec((tm, w), lambda i: (i, 0)),
                  row(d_skip),
                  pl.BlockSpec((None, w, w), lambda i: (layer, 0, 0)),
                  row(b_glu)],
        out_specs=pl.BlockSpec((tm, w), lambda i: (i, 0)),
        out_shape=jax.ShapeDtypeStruct((t, w), BF16),
        compiler_params=_params("parallel"),
    )(y, proj, d_skip.reshape(1, w), w_glu, b_glu.reshape(1, w))


CA_QBLOCK = 4 * CHUNK
CA_WINDOW = CA_LEFT_CHUNKS * CHUNK + CA_QBLOCK
CA_LEAD = CA_LEFT_CHUNKS * CHUNK // CA_QBLOCK


def _ca_bias(rel_bias):
    heads = rel_bias.shape[0]
    o = jnp.arange(CA_LEAD + 1)[:, None, None]
    q_pos = o * CA_QBLOCK + jnp.arange(CA_QBLOCK)[None, :, None]
    k_pos = jnp.arange(CA_WINDOW)[None, None, :]
    q_chunk, k_chunk = q_pos // CHUNK, k_pos // CHUNK
    allowed = (k_chunk <= q_chunk) & (k_chunk >= q_chunk - CA_LEFT_CHUNKS)
    period = CA_WINDOW + CA_QBLOCK
    n = jnp.arange(period)
    k_minus_q = jnp.where(n < CA_WINDOW, n, n - period)
    idx = jnp.clip(o[:, :, 0] * CA_QBLOCK - k_minus_q[None, :], -REL_CLIP, REL_CLIP) + REL_CLIP
    line = (LOG2E * rel_bias.astype(F32)[:, idx]).transpose(1, 0, 2)
    tiled = jnp.tile(line, (1, 1, CA_QBLOCK))[..., :CA_QBLOCK * (period - 1)]
    bias = tiled.reshape(CA_LEAD + 1, heads, CA_QBLOCK, period - 1)[..., :CA_WINDOW]
    return jnp.where(allowed[:, None], bias, MASK_VALUE)


def _ca_kernel(q_ref, k_ref, v_ref, bias_ref, qg_ref, kg_ref, o_ref, kn_ref, *, scale):
    kn_ref[...] = _rms(k_ref[...].astype(F32), kg_ref[...]).astype(BF16)
    n_blocks = q_ref.shape[0] // CA_QBLOCK

    def body(i, carry):
        r0 = pl.multiple_of(i * CA_QBLOCK, CA_QBLOCK)
        w0 = pl.multiple_of(jnp.maximum(i - CA_LEAD, 0) * CA_QBLOCK, CA_QBLOCK)
        q = (_rms(q_ref[pl.ds(r0, CA_QBLOCK), :].astype(F32), qg_ref[...]) * scale).astype(BF16)
        s = lax.dot_general(q, kn_ref[pl.ds(w0, CA_WINDOW), :], (((1,), (1,)), ((), ())),
                            preferred_element_type=F32)
        s = s + bias_ref[jnp.minimum(i, CA_LEAD), 0]
        p = jnp.exp2(s - jnp.max(s, axis=-1, keepdims=True))
        denom = jnp.sum(p, axis=-1, keepdims=True)
        out = jnp.dot(p.astype(BF16), v_ref[pl.ds(w0, CA_WINDOW), :], preferred_element_type=F32)
        o_ref[pl.ds(r0, CA_QBLOCK), :] = (out * (1.0 / denom)).astype(o_ref.dtype)
        return carry

    lax.fori_loop(0, n_blocks, body, 0, unroll=4)


def chunk_band_attention(proj, q_col, k_col, v_col, heads, bsz, seq, q_gain, k_gain, rel_bias):
    bias = _ca_bias(rel_bias)
    head_block = lambda col: pl.BlockSpec((seq, HEAD_DIM), lambda h, b: (b, col + h))
    gain = pl.BlockSpec((1, HEAD_DIM), lambda h, b: (0, 0))
    return pl.pallas_call(
        functools.partial(_ca_kernel, scale=LOG2E * HEAD_DIM ** -0.5),
        grid=(heads, bsz),
        in_specs=[head_block(q_col), head_block(k_col), head_block(v_col),
                  pl.BlockSpec((CA_LEAD + 1, 1, CA_QBLOCK, CA_WINDOW), lambda h, b: (0, h, 0, 0)),
                  gain, gain],
        out_specs=pl.BlockSpec((seq, HEAD_DIM), lambda h, b: (b, h)),
        out_shape=jax.ShapeDtypeStruct((bsz * seq, heads * HEAD_DIM), BF16),
        scratch_shapes=[pltpu.VMEM((seq, HEAD_DIM), BF16)],
        compiler_params=_params("parallel", "parallel"),
    )(proj, proj, proj, bias, q_gain.reshape(1, HEAD_DIM), k_gain.reshape(1, HEAD_DIM))


DA_BLOCK = 512


def _split3(x):
    hi = x.astype(BF16)
    rest = x - hi.astype(F32)
    mid = rest.astype(BF16)
    return hi, mid, (rest - mid.astype(F32)).astype(BF16)


def _da_kernel(q1_ref, q2_ref, k1_ref, k2_ref, v_ref, qg_ref, kg_ref, lam_ref, slope_ref, sg_ref,
               o_ref, kx_ref, *, scale, lambda_init):
    tq = DA_BLOCK
    seq = q1_ref.shape[0]
    n_blocks = seq // tq
    slope = slope_ref[0, :, 0:1]
    lam = (jnp.exp(jnp.sum(lam_ref[0:1, :] * lam_ref[1:2, :], axis=-1, keepdims=True))
           - jnp.exp(jnp.sum(lam_ref[2:3, :] * lam_ref[3:4, :], axis=-1, keepdims=True))
           + lambda_init)

    def extra_columns(shape, first, second):
        lane = lax.broadcasted_iota(jnp.int32, shape, 1)
        out = jnp.zeros(shape, F32)
        for offset, pieces in ((0, first), (3, second)):
            for n, piece in enumerate(pieces):
                piece = piece if isinstance(piece, float) else piece.astype(F32)
                out = jnp.where(lane == offset + n, piece, out)
        return out.astype(BF16)

    ones = (1.0, 1.0, 1.0)
    key_pos = lax.broadcasted_iota(jnp.int32, (seq, 1), 0).astype(F32)
    key_extra = extra_columns((seq, HEAD_DIM), _split3(slope * key_pos), ones)
    for c, k_ref in enumerate((k1_ref, k2_ref)):
        kx_ref[c, :, 0:HEAD_DIM] = _rms(k_ref[...].astype(F32), kg_ref[...]).astype(BF16)
        kx_ref[c, :, HEAD_DIM:] = key_extra

    row = lax.broadcasted_iota(jnp.int32, (tq, tq), 0)
    col = lax.broadcasted_iota(jnp.int32, (tq, tq), 1)
    ahead = jnp.where(col > row, (2.0 * slope) * (row - col).astype(F32), 0.0)
    allowed = (col // CHUNK) <= (row // CHUNK)

    def scores(qx, c, c0):
        return lax.dot_general(qx, kx_ref[c, pl.ds(c0, tq), :], (((1,), (1,)), ((), ())),
                               preferred_element_type=F32)

    def fold(a, op):
        parts = [a[:, n * HEAD_DIM:(n + 1) * HEAD_DIM] for n in range(tq // HEAD_DIM)]
        while len(parts) > 1:
            parts = [op(parts[n], parts[n + 1]) for n in range(0, len(parts), 2)]
        return parts[0]

    def past_blocks(n, step, state):
        pairs = n // 2
        state = lax.fori_loop(0, pairs, lambda t, st: step(2 * t + 1, step(2 * t, st)), state)
        return lax.fori_loop(2 * pairs, n, step, state)

    def q_block(i, carry):
        r0 = pl.multiple_of(i * tq, tq)
        qn = [(_rms(q_ref[pl.ds(r0, tq), :].astype(F32), qg_ref[...]) * scale).astype(BF16)
              for q_ref in (q1_ref, q2_ref)]
        zeros3 = (0.0, 0.0, 0.0)
        qx = [jnp.concatenate([q, extra_columns((tq, HEAD_DIM), ones, zeros3)], axis=1) for q in qn]

        def max_step(j, mx):
            c0 = pl.multiple_of(j * tq, tq)
            return tuple(jnp.maximum(mx[c], fold(scores(qx[c], c, c0), jnp.maximum))
                         for c in range(2))

        start = jnp.full((tq, HEAD_DIM), MASK_VALUE, F32)
        mx = past_blocks(i, max_step, (start, start))
        row_max = [jnp.maximum(
            jnp.max(mx[c], axis=-1, keepdims=True),
            jnp.max(jnp.where(allowed, scores(qx[c], c, r0) + ahead, MASK_VALUE),
                    axis=-1, keepdims=True)) for c in range(2)]
        qx = [jnp.concatenate([qn[c], extra_columns((tq, HEAD_DIM), ones, _split3(-row_max[c]))],
                              axis=1) for c in range(2)]

        def sum_step(j, st):
            c0 = pl.multiple_of(j * tq, tq)
            v = v_ref[pl.ds(c0, tq), :]
            out = []
            for c in range(2):
                p = jnp.exp2(scores(qx[c], c, c0))
                out.append((st[c][0] + fold(p, jnp.add),
                            st[c][1] + jnp.dot(p.astype(BF16), v, preferred_element_type=F32)))
            return tuple(out)

        init = (jnp.zeros((tq, HEAD_DIM), F32), jnp.zeros((tq, v_ref.shape[1]), F32))
        st = past_blocks(i, sum_step, (init, init))
        v = v_ref[pl.ds(r0, tq), :]
        outs = []
        for c in range(2):
            p = jnp.exp2(jnp.where(allowed, scores(qx[c], c, r0) + ahead, MASK_VALUE))
            denom = jnp.sum(st[c][0] + fold(p, jnp.add), axis=-1, keepdims=True)
            outs.append((st[c][1] + jnp.dot(p.astype(BF16), v, preferred_element_type=F32))
                        * (1.0 / denom))
        out = outs[0] - lam * outs[1]
        out = _rms(out, sg_ref[...]) * (1.0 - lambda_init)
        o_ref[pl.ds(r0, tq), :] = out.astype(o_ref.dtype)
        return carry

    lax.fori_loop(0, n_blocks, q_block, 0)


def diff_attention(proj, q_col, k_col, v_col, heads, bsz, seq, q_gain, k_gain, lam_vecs, subln_gain,
                   lambda_init):
    slopes = LOG2E * 2.0 ** (-8.0 * jnp.arange(1, heads + 1, dtype=F32) / heads)
    slopes = jnp.broadcast_to(slopes[:, None, None], (heads, 1, HEAD_DIM))
    qk_block = lambda col: pl.BlockSpec((seq, HEAD_DIM), lambda b, h: (b, col + 2 * h))
    vec = lambda n: pl.BlockSpec((1, n), lambda b, h: (0, 0))
    return pl.pallas_call(
        functools.partial(_da_kernel, scale=LOG2E * HEAD_DIM ** -0.5, lambda_init=lambda_init),
        grid=(bsz, heads),
        in_specs=[qk_block(q_col), qk_block(q_col + 1), qk_block(k_col), qk_block(k_col + 1),
                  pl.BlockSpec((seq, 2 * HEAD_DIM), lambda b, h: (b, v_col + h)),
                  vec(HEAD_DIM), vec(HEAD_DIM),
                  pl.BlockSpec((4, HEAD_DIM), lambda b, h: (0, 0)),
                  pl.BlockSpec((1, 1, HEAD_DIM), lambda b, h: (h, 0, 0)),
                  vec(2 * HEAD_DIM)],
        out_specs=pl.BlockSpec((seq, 2 * HEAD_DIM), lambda b, h: (b, h)),
        out_shape=jax.ShapeDtypeStruct((bsz * seq, heads * 2 * HEAD_DIM), BF16),
        scratch_shapes=[pltpu.VMEM((2, seq, 2 * HEAD_DIM), BF16)],
        compiler_params=_params("parallel", "parallel"),
    )(proj, proj, proj, proj, proj, q_gain.reshape(1, HEAD_DIM), k_gain.reshape(1, HEAD_DIM),
      lam_vecs, slopes, subln_gain.reshape(1, 2 * HEAD_DIM))


def _merge_kernel(ya_ref, yb_ref, yc_ref, wa_ref, wb_ref, wc_ref, ga_ref, gb_ref, gc_ref, o_ref):
    def branch(y_ref, w_ref, g_ref):
        out = jnp.dot(y_ref[...], w_ref[...], preferred_element_type=F32)
        return _sigmoid(g_ref[...].astype(F32)) * out

    merged = (branch(ya_ref, wa_ref, ga_ref) + branch(yb_ref, wb_ref, gb_ref)
              + branch(yc_ref, wc_ref, gc_ref))
    o_ref[...] = merged.astype(o_ref.dtype)


def gated_merge(ys, ws, layer, proj, gate_col, *, tm=1024, tn=1024):
    t = ys[0].shape[0]
    d = ws[0].shape[2]
    tm, tn = _tile(t, tm, align=8), _tile(math.gcd(d, gate_col), tn)
    y_spec = lambda y: pl.BlockSpec((tm, y.shape[1]), lambda j, i: (i, 0))
    w_spec = lambda w: pl.BlockSpec((None, w.shape[1], tn), lambda j, i: (layer, 0, j))
    g_spec = lambda b: pl.BlockSpec((tm, tn), lambda j, i: (i, (gate_col + b * d) // tn + j))
    return pl.pallas_call(
        _merge_kernel,
        grid=(d // tn, t // tm),
        in_specs=[y_spec(y) for y in ys] + [w_spec(w) for w in ws] + [g_spec(b) for b in range(3)],
        out_specs=pl.BlockSpec((tm, tn), lambda j, i: (i, j)),
        out_shape=jax.ShapeDtypeStruct((t, d), BF16),
        compiler_params=_params("parallel", "parallel"),
    )(*ys, *ws, proj, proj, proj)


def kernel(x, norm_mix, w_in, ssm_a_re, ssm_a_im, ssm_log_dt, ssm_b_re, ssm_b_im, ssm_c_re, ssm_c_im, ssm_d, ssm_w_glu, ssm_b_glu, ca_q_gain, ca_k_gain, ca_rel_bias, da_q_gain, da_k_gain, da_lam_q1, da_lam_k1, da_lam_q2, da_lam_k2, da_subln_gain, w_out_a, w_out_b, w_out_c, w_o, norm_mlp, w_ff1, w_ff2):
    bsz, seq, d_model = x.shape
    depth = w_in.shape[0]
    ssm_w, ca_w, da_w = w_out_a.shape[1], w_out_b.shape[1], w_out_c.shape[1]
    ca_heads, da_heads = ca_w // HEAD_DIM, da_w // (2 * HEAD_DIM)
    n_chunks = seq // SSM_CHUNK
    ca_q = ssm_w // HEAD_DIM
    ca_k, ca_v = ca_q + ca_heads, ca_q + 2 * ca_heads
    da_q = ca_q + 3 * ca_heads
    da_k, da_v = da_q + 2 * da_heads, da_q + 4 * da_heads
    gate_col = ssm_w + 3 * ca_w + 3 * da_w

    w_glu, w_o, w_ff2 = ssm_w_glu.astype(BF16), w_o.astype(BF16), w_ff2.astype(BF16)
    w_out = tuple(w.astype(BF16) for w in (w_out_a, w_out_b, w_out_c))
    stream_f32 = dict(rows_outer=True, tm=2048, tn=512)

    xt = x.reshape(bsz * seq, d_model)
    for l in range(depth):
        lambda_init = 0.8 - 0.6 * math.exp(-0.3 * l)
        h = rmsnorm(xt, norm_mix[l])
        proj = matmul(h, w_in, l, out_dtype=BF16, **stream_f32)

        mats = _s5_matrices(ssm_a_re[l], ssm_a_im[l], ssm_log_dt[l], ssm_b_re[l], ssm_b_im[l],
                            ssm_c_re[l], ssm_c_im[l], n_chunks)
        y_s5 = s5_scan(proj, mats, bsz, seq, ssm_w)
        y_a = s5_glu(y_s5, proj, ssm_d[l], w_glu, l, ssm_b_glu[l])

        y_b = chunk_band_attention(proj, ca_q, ca_k, ca_v, ca_heads, bsz, seq,
                                   ca_q_gain[l], ca_k_gain[l], ca_rel_bias[l])
        lam_vecs = jnp.stack([da_lam_q1[l], da_lam_k1[l], da_lam_q2[l], da_lam_k2[l]]).astype(F32)
        y_c = diff_attention(proj, da_q, da_k, da_v // 2, da_heads, bsz, seq,
                             da_q_gain[l], da_k_gain[l], lam_vecs, da_subln_gain[l], lambda_init)

        merged = gated_merge((y_a, y_b, y_c), w_out, l, proj, gate_col)
        xt = matmul(merged, w_o, l, out_dtype=F32, epilogue="residual", residual=xt)
        h = rmsnorm(xt, norm_mlp[l])
        ff = matmul(h, w_ff1, l, out_dtype=BF16, epilogue="relu2", **stream_f32)
        xt = matmul(ff, w_ff2, l, out_dtype=F32, epilogue="residual", residual=xt, tk=4096)
    return xt.reshape(bsz, seq, d_model)
```
